```python
import jax, jax.numpy as jnp
from jax import lax
import numpy as np

D_MODEL = 1024
BATCH = 8
SEQ = 2048
DEPTH = 1
DEC_BATCH = 4
DEC_SEQ = 4096
PAST_LEN = 128

N_META = 16
EPS = 1e-6
D_FF = 2816
SSD_D_INNER = D_MODEL
SSD_HEADDIM = 64
SSD_HEADS = SSD_D_INNER // SSD_HEADDIM
SSD_GROUPS = 4
SSD_STATE = 128
SSD_CONV = 5
SSD_CHUNK = 128
SSD_GN = SSD_GROUPS * SSD_STATE
SSD_CONV_DIM = SSD_D_INNER + 2 * SSD_GN
HG_WIDTH = D_MODEL
HG_HEADDIM = 128
HG_HEADS = HG_WIDTH // HG_HEADDIM
HG_CHUNK = 64
IN_WIDTHS = (SSD_D_INNER, SSD_CONV_DIM, 2 * SSD_HEADS, HG_WIDTH, 2 * HG_WIDTH, HG_WIDTH, HG_WIDTH, 2 * D_MODEL)
IN_COLS = sum(IN_WIDTHS)

kernel_name = 'hybrid_ssd_hgrn2_bidir_encoder'


def _split_points(widths):
    pts, acc = [], 0
    for w in widths[:-1]:
        acc += w
        pts.append(acc)
    return pts


def rmsnorm(x, w):
    xf = x.astype(jnp.float32)
    y = xf * lax.rsqrt(jnp.mean(xf * xf, axis=-1, keepdims=True) + EPS)
    return (y * w.astype(jnp.float32)).astype(x.dtype)


def swiglu(h, w_gate_up, w_down):
    g, u = jnp.split(h @ w_gate_up, 2, axis=-1)
    return (jax.nn.silu(g) * u) @ w_down


def _pad_front(u, n):
    return jnp.pad(u, [(0, 0), (n, 0)] + [(0, 0)] * (u.ndim - 2))


def _flip(u):
    return jnp.flip(u, axis=1)


def depthwise_conv_centred(u, w, bias):
    c = u.shape[-1]
    y = lax.conv_general_dilated(u, w.reshape(SSD_CONV, 1, c), window_strides=(1,),
                                 padding=[(SSD_CONV // 2, SSD_CONV // 2)],
                                 dimension_numbers=('NWC', 'WIO', 'NWC'), feature_group_count=c)
    return y + bias


def ssd_chunked(x, dt, A, Bm, Cm):
    b, T, H, P = x.shape
    G, N = Bm.shape[2], Bm.shape[3]
    Hg = H // G
    Q = SSD_CHUNK
    c = T // Q
    acs = jnp.cumsum((dt * A).reshape(b, c, Q, G, Hg), axis=2)
    xdt = (x * dt[..., None]).reshape(b, c, Q, G, Hg, P)
    Bc = Bm.reshape(b, c, Q, G, N)
    Cc = Cm.reshape(b, c, Q, G, N)
    mask = jnp.tril(jnp.ones((Q, Q), dtype=bool))[:, :, None, None]
    seg = acs[:, :, :, None] - acs[:, :, None, :]
    Lm = jnp.exp(jnp.where(mask, seg, -jnp.inf))
    CB = jnp.einsum('bctgn,bcsgn->bctsg', Cc, Bc)
    y_diag = jnp.einsum('bctsg,bctsgh,bcsghp->bctghp', CB, Lm, xdt)
    decay_to_end = jnp.exp(acs[:, :, -1:] - acs)
    states = jnp.einsum('bcsgn,bcsgh,bcsghp->bcghpn', Bc, decay_to_end, xdt)
    chunk_decay = jnp.exp(acs[:, :, -1])

    def step(h, inp):
        st, dec = inp
        return dec[..., None, None] * h + st, h

    h0 = jnp.zeros((b, G, Hg, P, N), x.dtype)
    _, prev = lax.scan(step, h0, (jnp.moveaxis(states, 1, 0), jnp.moveaxis(chunk_decay, 1, 0)))
    prev = jnp.moveaxis(prev, 0, 1)
    y_off = jnp.einsum('bctgn,bcghpn,bctgh->bctghp', Cc, prev, jnp.exp(acs))
    return (y_diag + y_off).reshape(b, T, H, P)


def hgrn2_chunked(q, k, v, logf):
    b, T, H, K = q.shape
    V = v.shape[-1]
    Q = HG_CHUNK
    c = T // Q
    q = q.reshape(b, c, Q, H, K)
    k = k.reshape(b, c, Q, H, K)
    v = v.reshape(b, c, Q, H, V)
    bcs = jnp.cumsum(logf.reshape(b, c, Q, H, K), axis=2)
    ref = bcs[:, :, Q // 2:Q // 2 + 1]
    qe = q * jnp.exp(bcs - ref)
    ke = k * jnp.exp(ref - bcs)
    mask = jnp.tril(jnp.ones((Q, Q), dtype=bool))
    A = jnp.where(mask, jnp.einsum('bcthk,bcshk->bchts', qe, ke), 0.0)
    o_intra = jnp.einsum('bchts,bcshv->bcthv', A, v)
    kv = jnp.einsum('bcshk,bcshv->bchkv', k * jnp.exp(bcs[:, :, -1:] - bcs), v)
    chunk_decay = jnp.exp(bcs[:, :, -1])

    def step(S, inp):
        upd, dec = inp
        return dec[..., None] * S + upd, S

    S0 = jnp.zeros((b, H, K, V), q.dtype)
    _, prev = lax.scan(step, S0, (jnp.moveaxis(kv, 1, 0), jnp.moveaxis(chunk_decay, 1, 0)))
    prev = jnp.moveaxis(prev, 0, 1)
    o_inter = jnp.einsum('bcthk,bchkv->bcthv', q * jnp.exp(bcs), prev)
    return (o_intra + o_inter).reshape(b, T, H, V)


def ssd_branch(z, xbc, dt_raw, conv_w, conv_b, dt_bias, a_log, d_skip, norm_w):
    f32 = jnp.float32
    b, L, _ = xbc.shape
    xbc = jax.nn.silu(depthwise_conv_centred(xbc.astype(f32), conv_w.astype(f32), conv_b.astype(f32)))
    xs = xbc[..., :SSD_D_INNER].reshape(b, L, SSD_HEADS, SSD_HEADDIM)
    Bm = xbc[..., SSD_D_INNER:SSD_D_INNER + SSD_GN].reshape(b, L, SSD_GROUPS, SSD_STATE)
    Cm = xbc[..., SSD_D_INNER + SSD_GN:].reshape(b, L, SSD_GROUPS, SSD_STATE)
    dt = jax.nn.softplus(dt_raw.astype(f32).reshape(b, L, 2, SSD_HEADS) + dt_bias.astype(f32))
    A = -jnp.exp(a_log.astype(f32))
    npad = SSD_CHUNK - N_META
    xp, Bp, Cp, dtp = (_pad_front(u, npad) for u in (xs, Bm, Cm, dt))
    y_fwd = ssd_chunked(xp, dtp[:, :, 0], A[0], Bp, Cp)
    y_bwd = _flip(ssd_chunked(_flip(xp), _flip(dtp[:, :, 1]), A[1], _flip(Bp), _flip(Cp)))
    y = (y_fwd + y_bwd)[:, npad:] + d_skip.astype(f32)[:, None] * xs
    y = y.reshape(b, L, SSD_D_INNER) * jax.nn.silu(z.astype(f32))
    return rmsnorm(y, norm_w)


def hgrn2_branch(hq, hf, hi, hg, lb, norm_w):
    f32 = jnp.float32
    b, L, _ = hq.shape
    q = jax.nn.silu(hq.astype(f32)).reshape(b, L, HG_HEADS, HG_HEADDIM)
    v = hi.astype(f32).reshape(b, L, HG_HEADS, HG_HEADDIM)
    lbf = lb.astype(f32).reshape(2, HG_HEADS, HG_HEADDIM)
    f = lbf + (1.0 - lbf) * jax.nn.sigmoid(hf.astype(f32).reshape(b, L, 2, HG_HEADS, HG_HEADDIM))
    k = 1.0 - f
    logf = jnp.log(f)
    npad = HG_CHUNK - N_META
    qp, vp, kp, lfp = (_pad_front(u, npad) for u in (q, v, k, logf))
    o_fwd = hgrn2_chunked(qp, kp[:, :, 0], vp, lfp[:, :, 0])
    o_bwd = _flip(hgrn2_chunked(_flip(qp), _flip(kp[:, :, 1]), _flip(vp), _flip(lfp[:, :, 1])))
    o = (o_fwd + o_bwd)[:, npad:]
    o = o * lax.rsqrt(jnp.mean(o * o, axis=-1, keepdims=True) + EPS) * norm_w.astype(f32).reshape(HG_HEADS, HG_HEADDIM)
    return o.reshape(b, L, HG_WIDTH) * jax.nn.silu(hg.astype(f32))


def hybrid_mixer(h, w_in, conv_w, conv_b, dt_bias, a_log, d_skip, ssd_norm, ssd_w_proj,
                 lb, hg_norm, hg_w_proj, w_out):
    proj = h @ w_in
    z, xbc, dt_raw, hq, hf, hi, hg, gates = jnp.split(proj, _split_points(IN_WIDTHS), axis=-1)
    branch_a = ssd_branch(z, xbc, dt_raw, conv_w, conv_b, dt_bias, a_log, d_skip, ssd_norm) @ ssd_w_proj
    branch_b = hgrn2_branch(hq, hf, hi, hg, lb, hg_norm) @ hg_w_proj
    ga, gb = jnp.split(jax.nn.sigmoid(gates.astype(jnp.float32)), 2, axis=-1)
    merged = ga * branch_a + gb * branch_b
    return (merged @ w_out).astype(h.dtype)


def encoder_trunk(x, meta_tokens, ffn1_norm, ffn1_w_gate_up, ffn1_w_down, mix_norm, w_in,
                  ssd_conv_w, ssd_conv_b, ssd_dt_bias, ssd_a_log, ssd_d, ssd_norm, ssd_w_proj,
                  hg_lb_table, hg_norm, hg_w_proj, w_out, ffn2_norm, ffn2_w_gate_up, ffn2_w_down,
                  final_norm):
    b = x.shape[0]
    meta = jnp.broadcast_to(meta_tokens[None].astype(x.dtype), (b, N_META, x.shape[-1]))
    h = jnp.concatenate([meta, x], axis=1)
    lb_all = jnp.cumsum(jax.nn.softmax(hg_lb_table.astype(jnp.float32), axis=1), axis=1)
    for l in range(DEPTH):
        h = h + 0.5 * swiglu(rmsnorm(h, ffn1_norm[l]), ffn1_w_gate_up[l], ffn1_w_down[l])
        h = h + hybrid_mixer(rmsnorm(h, mix_norm[l]), w_in[l], ssd_conv_w[l], ssd_conv_b[l],
                             ssd_dt_bias[l], ssd_a_log[l], ssd_d[l], ssd_norm[l], ssd_w_proj[l],
                             lb_all[:, l], hg_norm[l], hg_w_proj[l], w_out[l])
        h = h + 0.5 * swiglu(rmsnorm(h, ffn2_norm[l]), ffn2_w_gate_up[l], ffn2_w_down[l])
    return rmsnorm(h, final_norm)[:, N_META:]


def setup_inputs(seed: int = 0) -> dict:
    key = jax.random.key(seed)
    ks = jax.random.split(key, 26)
    nrm = lambda k, shape, scale: jax.random.normal(k, shape, jnp.float32) * scale
    gain = lambda k, shape: 1.0 + 0.05 * jax.random.normal(k, shape, jnp.float32)
    dt0 = jnp.exp(jax.random.uniform(ks[0], (DEPTH, 2, SSD_HEADS), jnp.float32, np.log(1e-3), np.log(1e-1)))
    return {
        'x_prompt': nrm(ks[1], (BATCH, SEQ, D_MODEL), 1.0),
        'x_sample': nrm(ks[2], (DEC_BATCH, DEC_SEQ, D_MODEL), 1.0),
        'meta_tokens': nrm(ks[3], (N_META, D_MODEL), 1.0),
        'ffn1_norm': gain(ks[4], (DEPTH, D_MODEL)),
        'ffn1_w_gate_up': nrm(ks[5], (DEPTH, D_MODEL, 2 * D_FF), D_MODEL ** -0.5),
        'ffn1_w_down': nrm(ks[6], (DEPTH, D_FF, D_MODEL), D_FF ** -0.5),
        'mix_norm': gain(ks[7], (DEPTH, D_MODEL)),
        'w_in': nrm(ks[8], (DEPTH, D_MODEL, IN_COLS), D_MODEL ** -0.5),
        'ssd_conv_w': nrm(ks[9], (DEPTH, SSD_CONV, SSD_CONV_DIM), SSD_CONV ** -0.5),
        'ssd_conv_b': nrm(ks[10], (DEPTH, SSD_CONV_DIM), 0.02),
        'ssd_dt_bias': dt0 + jnp.log(-jnp.expm1(-dt0)),
        'ssd_a_log': jnp.log(jax.random.uniform(ks[11], (DEPTH, 2, SSD_HEADS), jnp.float32, 1.0, 16.0)),
        'ssd_d': gain(ks[12], (DEPTH, SSD_HEADS)),
        'ssd_norm': gain(ks[13], (DEPTH, SSD_D_INNER)),
        'ssd_w_proj': nrm(ks[14], (DEPTH, SSD_D_INNER, D_MODEL), SSD_D_INNER ** -0.5),
        'hg_lb_table': nrm(ks[15], (2, DEPTH + 1, HG_WIDTH), 0.1),
        'hg_norm': gain(ks[16], (DEPTH, HG_WIDTH)),
        'hg_w_proj': nrm(ks[17], (DEPTH, HG_WIDTH, D_MODEL), HG_WIDTH ** -0.5),
        'w_out': nrm(ks[18], (DEPTH, D_MODEL, D_MODEL), D_MODEL ** -0.5),
        'ffn2_norm': gain(ks[19], (DEPTH, D_MODEL)),
        'ffn2_w_gate_up': nrm(ks[20], (DEPTH, D_MODEL, 2 * D_FF), D_MODEL ** -0.5),
        'ffn2_w_down': nrm(ks[21], (DEPTH, D_FF, D_MODEL), D_FF ** -0.5),
        'final_norm': gain(ks[22], (D_MODEL,)),
    }


def reference(x_prompt, x_sample, meta_tokens, ffn1_norm, ffn1_w_gate_up, ffn1_w_down, mix_norm, w_in,
              ssd_conv_w, ssd_conv_b, ssd_dt_bias, ssd_a_log, ssd_d, ssd_norm, ssd_w_proj,
              hg_lb_table, hg_norm, hg_w_proj, w_out, ffn2_norm, ffn2_w_gate_up, ffn2_w_down, final_norm):
    y_prompt = encoder_trunk(x_prompt, meta_tokens, ffn1_norm, ffn1_w_gate_up, ffn1_w_down, mix_norm, w_in,
                             ssd_conv_w, ssd_conv_b, ssd_dt_bias, ssd_a_log, ssd_d, ssd_norm, ssd_w_proj,
                             hg_lb_table, hg_norm, hg_w_proj, w_out, ffn2_norm, ffn2_w_gate_up, ffn2_w_down,
                             final_norm)
    y_sample = encoder_trunk(x_sample, meta_tokens, ffn1_norm, ffn1_w_gate_up, ffn1_w_down, mix_norm, w_in,
                             ssd_conv_w, ssd_conv_b, ssd_dt_bias, ssd_a_log, ssd_d, ssd_norm, ssd_w_proj,
                             hg_lb_table, hg_norm, hg_w_proj, w_out, ffn2_norm, ffn2_w_gate_up, ffn2_w_down,
                             final_norm)
    return (y_prompt, y_sample)
```

```python
import functools

import numpy as np
import jax
import jax.numpy as jnp
from jax import lax
from jax.experimental import pallas as pl
from jax.experimental.pallas import tpu as pltpu

F32 = jnp.float32
BF16 = jnp.bfloat16

D_MODEL = 1024
N_META = 16
EPS = 1e-6
D_FF = 2816
SSD_HEADS = 16
SSD_HEADDIM = 64
SSD_GROUPS = 4
SSD_STATE = 128
SSD_CONV = 5
SSD_GN = SSD_GROUPS * SSD_STATE
SSD_CONV_DIM = D_MODEL + 2 * SSD_GN
HG_HEADS = 8
HG_HEADDIM = 128
HG_CHUNK = 64
IN_WIDTHS = (D_MODEL, SSD_CONV_DIM, 2 * SSD_HEADS, D_MODEL, 2 * D_MODEL, D_MODEL, D_MODEL, 2 * D_MODEL)

TILE = 128
ROW_TILE = 512
HALO_ROWS = 16
LANES = 128
VMEM_LIMIT = 56 * 1024 * 1024


def _rms(x, w):
    return x * lax.rsqrt(jnp.mean(x * x, axis=-1, keepdims=True) + EPS) * w


def _dot(a, b):
    return jnp.dot(a, b, preferred_element_type=F32)


def _dot_nt(a, b):
    return lax.dot_general(a, b, (((1,), (1,)), ((), ())), preferred_element_type=F32)


def _dot_tn(a, b):
    return lax.dot_general(a, b, (((0,), (0,)), ((), ())), preferred_element_type=F32)


def _dot_split3(m01, x):
    x1 = x.astype(BF16)
    r1 = x - x1.astype(F32)
    x2 = r1.astype(BF16)
    x3 = (r1 - x2.astype(F32)).astype(BF16)
    return _dot(m01, x1) + _dot(m01, x2) + _dot(m01, x3)


def _ffn_body(x_ref, nw_ref, wg_ref, wu_ref, wd_ref, fnw_ref, o_ref, *, final, n_chunks):
    x = x_ref[...]
    hn = _rms(x, nw_ref[...]).astype(BF16)
    tf = D_FF // n_chunks
    acc = jnp.zeros(x.shape, F32)
    for c in range(n_chunks):
        g = _dot(hn, wg_ref[:, c * tf:(c + 1) * tf])
        u = _dot(hn, wu_ref[:, c * tf:(c + 1) * tf])
        a = (jax.nn.silu(g) * u).astype(BF16)
        acc = acc + _dot(a, wd_ref[c * tf:(c + 1) * tf, :])
    out = x + 0.5 * acc
    if final:
        out = _rms(out, fnw_ref[...])
    o_ref[...] = out


def _resident(shape):
    return pl.BlockSpec(shape, lambda i: (0,) * len(shape), pipeline_mode=pl.Buffered(1))


def _ffn(x, nw, wg, wu, wd, fnw, *, final):
    t = x.shape[0]
    row = pl.BlockSpec((ROW_TILE, D_MODEL), lambda i: (i, 0))
    return pl.pallas_call(
        functools.partial(_ffn_body, final=final, n_chunks=2),
        grid=(t // ROW_TILE,),
        in_specs=[row, _resident((1, D_MODEL)), _resident((D_MODEL, D_FF)), _resident((D_MODEL, D_FF)),
                  _resident((D_FF, D_MODEL)), _resident((1, D_MODEL))],
        out_specs=row,
        out_shape=jax.ShapeDtypeStruct((t, D_MODEL), F32),
        compiler_params=pltpu.CompilerParams(dimension_semantics=("parallel",), vmem_limit_bytes=VMEM_LIMIT),
        name="ffn_final" if final else "ffn",
    )(x, nw, wg, wu, wd, fnw)


_C_Z, _C_XBC, _C_Q, _C_F, _C_V, _C_G, _C_GATE, _C_DT, _C_END = (
    0, 1024, 3072, 4096, 6144, 7168, 8192, 10240, 10368)


def _inproj_body(h_ref, nw_ref, w_ref, lb_ref, dtb_ref,
                 sz_ref, xbc_ref, q_ref, k_ref, lf_ref, v_ref, sg_ref, gate_ref, dt_ref):
    hn = _rms(h_ref[...], nw_ref[...]).astype(BF16)

    def proj(lo, hi):
        return _dot(hn, w_ref[:, lo:hi])

    sz_ref[...] = jax.nn.silu(proj(_C_Z, _C_XBC)).astype(BF16)
    xbc_ref[...] = proj(_C_XBC, _C_Q).astype(BF16)
    q_ref[...] = jax.nn.silu(proj(_C_Q, _C_F)).astype(BF16)
    lb = lb_ref[...]
    f = lb + (1.0 - lb) * jax.nn.sigmoid(proj(_C_F, _C_V))
    k_ref[...] = (1.0 - f).astype(BF16)
    lf_ref[...] = jnp.log(f)
    v_ref[...] = proj(_C_V, _C_G).astype(BF16)
    sg_ref[...] = jax.nn.silu(proj(_C_G, _C_GATE)).astype(BF16)
    gate_ref[...] = jax.nn.sigmoid(proj(_C_GATE, _C_DT)).astype(BF16)
    dt_ref[...] = jax.nn.softplus(proj(_C_DT, _C_END) + dtb_ref[...])


def _inproj(h, nw, w, lb, dtb):
    t = h.shape[0]

    def row(width):
        return pl.BlockSpec((ROW_TILE, width), lambda i: (i, 0))

    widths = (D_MODEL, SSD_CONV_DIM, D_MODEL, 2 * D_MODEL, 2 * D_MODEL, D_MODEL, D_MODEL, 2 * D_MODEL, LANES)
    dtypes = (BF16, BF16, BF16, BF16, F32, BF16, BF16, BF16, F32)
    return pl.pallas_call(
        _inproj_body,
        grid=(t // ROW_TILE,),
        in_specs=[row(D_MODEL), _resident((1, D_MODEL)), _resident((D_MODEL, _C_END)),
                  _resident((1, 2 * D_MODEL)), _resident((1, LANES))],
        out_specs=[row(wd) for wd in widths],
        out_shape=[jax.ShapeDtypeStruct((t, wd), dt) for wd, dt in zip(widths, dtypes)],
        compiler_params=pltpu.CompilerParams(dimension_semantics=("parallel",), vmem_limit_bytes=VMEM_LIMIT),
        name="inproj",
    )(h, nw, w, lb, dtb)


def _scan_body(first_ref, last_ref,
               xprev_ref, xbc_ref, xnext_ref, dt_ref, q_ref, k_ref, v_ref, lf_ref,
               cw_ref, cb_ref, arow_ref, e_ref, *rest, bwd, n_tiles):
    if bwd:
        (yf_ref, of_ref, sz_ref, sg_ref, dexp_ref, snorm_ref, hnorm_ref,
         bra_ref, brb_ref, s_ssd, s_hg) = rest
    else:
        y_ref, o_ref, s_ssd, s_hg = rest

    i = pl.program_id(0)
    t = (n_tiles - 1 - i) if bwd else i
    first = first_ref[t] == 1
    last = last_ref[t] == 1

    @pl.when(last if bwd else first)
    def _():
        s_ssd[...] = jnp.zeros_like(s_ssd)
        s_hg[...] = jnp.zeros_like(s_hg)

    row = lax.broadcasted_iota(jnp.int32, (TILE, 1), 0)
    valid = jnp.where(jnp.logical_and(first, row < TILE - N_META), 0.0, 1.0)

    keep_prev = jnp.where(first, 0.0, 1.0)
    keep_next = jnp.where(last, 0.0, 1.0)
    half = SSD_CONV // 2
    prev = xprev_ref[...].astype(F32)[HALO_ROWS - half:, :] * keep_prev
    nxt = xnext_ref[...].astype(F32)[:half, :] * keep_next
    ext = jnp.concatenate([prev, xbc_ref[...].astype(F32), nxt], axis=0)
    conv = cb_ref[...] + cw_ref[0:1, :] * ext[0:TILE]
    for j in range(1, SSD_CONV):
        conv = conv + cw_ref[j:j + 1, :] * ext[j:j + TILE]
    act = jax.nn.silu(conv) * valid
    xs = act[:, :D_MODEL]
    bm = act[:, D_MODEL:D_MODEL + SSD_GN].astype(BF16)
    cm = act[:, D_MODEL + SSD_GN:].astype(BF16)

    r_i = lax.broadcasted_iota(jnp.int32, (TILE, TILE), 0)
    c_i = lax.broadcasted_iota(jnp.int32, (TILE, TILE), 1)
    causal = (r_i <= c_i) if bwd else (r_i >= c_i)
    tri = jnp.where(causal, 1.0, 0.0).astype(BF16)
    end_row = 0 if bwd else TILE - 1
    dtv = dt_ref[...] * valid
    acs = _dot_split3(tri, dtv * arow_ref[...])
    total = acs[end_row:end_row + 1, :]
    e_in = jnp.exp(acs)
    e_end = jnp.exp(total - acs)
    expand = e_ref[...]
    dt_x = _dot(dtv.astype(BF16), expand)
    w_x = _dot((dtv * e_end).astype(BF16), expand)
    ein_x = _dot(e_in.astype(BF16), expand)
    etot_x = _dot_split3_rows(jnp.exp(total), expand)
    xdt = (xs * dt_x).astype(BF16)
    xdtd = (xs * w_x).astype(BF16)
    acs_t = acs.T

    lane_i = lax.broadcasted_iota(jnp.int32, (TILE, LANES), 1)
    lo_half = lane_i < SSD_HEADDIM
    s_old = s_ssd[...]
    s_old_b = s_old.astype(BF16)
    neg_inf = jnp.float32(-jnp.inf)
    hpg = SSD_HEADS // SSD_GROUPS
    gw = hpg * SSD_HEADDIM
    y_parts = []
    for g in range(SSD_GROUPS):
        bg = bm[:, g * SSD_STATE:(g + 1) * SSD_STATE]
        cg = cm[:, g * SSD_STATE:(g + 1) * SSD_STATE]
        cb = _dot_nt(cg, bg)
        for pair in range(hpg // 2):
            ms = []
            for hh in range(2):
                lane = (16 if bwd else 0) + g * hpg + pair * 2 + hh
                seg = acs[:, lane:lane + 1] - acs_t[lane:lane + 1, :]
                ms.append((cb * jnp.exp(jnp.where(causal, seg, neg_inf))).astype(BF16))
            c0 = g * gw + pair * LANES
            xp = xdt[:, c0:c0 + LANES]
            zero = jnp.zeros_like(xp)
            rhs = jnp.concatenate([jnp.where(lo_half, xp, zero), jnp.where(lo_half, zero, xp)], axis=0)
            y_parts.append(_dot(jnp.concatenate(ms, axis=1), rhs))
    y_diag = jnp.concatenate(y_parts, axis=1)
    y_off_parts = []
    for g in range(SSD_GROUPS):
        bg = bm[:, g * SSD_STATE:(g + 1) * SSD_STATE]
        cg = cm[:, g * SSD_STATE:(g + 1) * SSD_STATE]
        y_off_parts.append(_dot(cg, s_old_b[:, g * gw:(g + 1) * gw]))
        upd = _dot_tn(bg, xdtd[:, g * gw:(g + 1) * gw])
        s_ssd[:, g * gw:(g + 1) * gw] = s_old[:, g * gw:(g + 1) * gw] * etot_x[:, g * gw:(g + 1) * gw] + upd
    y_dir = y_diag + jnp.concatenate(y_off_parts, axis=1) * ein_x

    q_all = q_ref[...].astype(F32)
    k_all = k_ref[...].astype(F32) * valid
    lf_all = lf_ref[...] * valid
    v_all = v_ref[...]
    cq = HG_CHUNK
    r64 = lax.broadcasted_iota(jnp.int32, (cq, cq), 0)
    c64 = lax.broadcasted_iota(jnp.int32, (cq, cq), 1)
    causal64 = (r64 <= c64) if bwd else (r64 >= c64)
    tri64 = jnp.where(causal64, 1.0, 0.0).astype(BF16)
    ref_row = (cq - 1 - cq // 2) if bwd else cq // 2
    end64 = 0 if bwd else cq - 1
    o_chunks = [None, None]
    for step in range(TILE // cq):
        ci = (TILE // cq - 1 - step) if bwd else step
        rows = slice(ci * cq, (ci + 1) * cq)
        qc, kc, vc = q_all[rows], k_all[rows], v_all[rows]
        bcs = _dot_split3(tri64, lf_all[rows])
        bref = bcs[ref_row:ref_row + 1, :]
        btot = bcs[end64:end64 + 1, :]
        qe = (qc * jnp.exp(bcs - bref)).astype(BF16)
        ke = (kc * jnp.exp(bref - bcs)).astype(BF16)
        kd = (kc * jnp.exp(btot - bcs)).astype(BF16)
        qd = (qc * jnp.exp(bcs)).astype(BF16)
        st = s_hg[...]
        st_b = st.astype(BF16)
        decay = jnp.exp(btot)
        o_parts = []
        for h in range(HG_HEADS):
            cols = slice(h * HG_HEADDIM, (h + 1) * HG_HEADDIM)
            a = jnp.where(causal64, _dot_nt(qe[:, cols], ke[:, cols]), 0.0).astype(BF16)
            o_h = _dot(a, vc[:, cols]) + _dot_nt(qd[:, cols], st_b[:, cols])
            o_parts.append(o_h)
            s_hg[:, cols] = st[:, cols] * decay[:, cols] + _dot_tn(vc[:, cols], kd[:, cols])
        o_chunks[ci] = jnp.concatenate(o_parts, axis=1)
    o_dir = jnp.concatenate(o_chunks, axis=0)

    if not bwd:
        y_ref[...] = y_dir
        o_ref[...] = o_dir
        return

    y = (yf_ref[...] + y_dir + dexp_ref[...] * xs) * sz_ref[...].astype(F32)
    bra_ref[...] = _rms(y, snorm_ref[...]).astype(BF16)
    o = of_ref[...] + o_dir
    o_parts = []
    for h in range(HG_HEADS):
        oh = o[:, h * HG_HEADDIM:(h + 1) * HG_HEADDIM]
        o_parts.append(oh * lax.rsqrt(jnp.mean(oh * oh, axis=-1, keepdims=True) + EPS))
    o = jnp.concatenate(o_parts, axis=1) * hnorm_ref[...] * sg_ref[...].astype(F32)
    brb_ref[...] = o.astype(BF16)


def _dot_split3_rows(x_row, expand):
    x8 = jnp.broadcast_to(x_row, (8, x_row.shape[1]))
    return _dot_split3_lhs(x8, expand)[0:1, :]


def _dot_split3_lhs(x, m01):
    x1 = x.astype(BF16)
    r1 = x - x1.astype(F32)
    x2 = r1.astype(BF16)
    x3 = (r1 - x2.astype(F32)).astype(BF16)
    return _dot(x1, m01) + _dot(x2, m01) + _dot(x3, m01)


def _scan(first, last, xbc, dt, q, k, v, lf, cw, cb, arow, expand, extra, *, bwd):
    t = xbc.shape[0]
    n_tiles = t // TILE
    n_halo = t // HALO_ROWS
    per_tile = TILE // HALO_ROWS

    def tile_of(i):
        return (n_tiles - 1 - i) if bwd else i

    def rows(width, col=0):
        return pl.BlockSpec((TILE, width), lambda i, *_: (tile_of(i), col))

    def const(shape):
        return pl.BlockSpec(shape, lambda i, *_: (0,) * len(shape))

    d = 1 if bwd else 0
    in_specs = [
        pl.BlockSpec((HALO_ROWS, SSD_CONV_DIM), lambda i, *_: (jnp.maximum(tile_of(i) * per_tile - 1, 0), 0)),
        rows(SSD_CONV_DIM),
        pl.BlockSpec((HALO_ROWS, SSD_CONV_DIM),
                     lambda i, *_: (jnp.minimum((tile_of(i) + 1) * per_tile, n_halo - 1), 0)),
        rows(LANES), rows(D_MODEL), rows(D_MODEL, d), rows(D_MODEL), rows(D_MODEL, d),
        const((8, SSD_CONV_DIM)), const((1, SSD_CONV_DIM)), const((1, LANES)), const((LANES, D_MODEL)),
    ]
    args = [xbc, xbc, xbc, dt, q, k, v, lf, cw, cb, arow, expand]
    if bwd:
        in_specs += [rows(D_MODEL), rows(D_MODEL), rows(D_MODEL), rows(D_MODEL),
                     const((1, D_MODEL)), const((1, D_MODEL)), const((1, D_MODEL))]
        args += list(extra)
        out_dtype = BF16
    else:
        out_dtype = F32
    grid_spec = pltpu.PrefetchScalarGridSpec(
        num_scalar_prefetch=2,
        grid=(n_tiles,),
        in_specs=in_specs,
        out_specs=[rows(D_MODEL), rows(D_MODEL)],
        scratch_shapes=[pltpu.VMEM((SSD_STATE, D_MODEL), F32), pltpu.VMEM((HG_HEADDIM, D_MODEL), F32)],
    )
    return pl.pallas_call(
        functools.partial(_scan_body, bwd=bwd, n_tiles=n_tiles),
        grid_spec=grid_spec,
        out_shape=[jax.ShapeDtypeStruct((t, D_MODEL), out_dtype)] * 2,
        compiler_params=pltpu.CompilerParams(dimension_semantics=("arbitrary",), vmem_limit_bytes=VMEM_LIMIT),
        name="scan_bwd" if bwd else "scan_fwd",
    )(first, last, *args)


def _merge_body(bra_ref, brb_ref, gate_ref, h_ref, wa_ref, wb_ref, wo_ref, o_ref):
    a = _dot(bra_ref[...], wa_ref[...])
    b = _dot(brb_ref[...], wb_ref[...])
    gates = gate_ref[...].astype(F32)
    merged = gates[:, :D_MODEL] * a + gates[:, D_MODEL:] * b
    o_ref[...] = h_ref[...] + _dot(merged.astype(BF16), wo_ref[...])


def _merge(bra, brb, gates, h, wa, wb, wo):
    t = h.shape[0]

    def row(width):
        return pl.BlockSpec((ROW_TILE, width), lambda i: (i, 0))

    sq = _resident((D_MODEL, D_MODEL))
    return pl.pallas_call(
        _merge_body,
        grid=(t // ROW_TILE,),
        in_specs=[row(D_MODEL), row(D_MODEL), row(2 * D_MODEL), row(D_MODEL), sq, sq, sq],
        out_specs=row(D_MODEL),
        out_shape=jax.ShapeDtypeStruct((t, D_MODEL), F32),
        compiler_params=pltpu.CompilerParams(dimension_semantics=("parallel",), vmem_limit_bytes=VMEM_LIMIT),
        name="merge",
    )(bra, brb, gates, h, wa, wb, wo)


def _tile_flags(seq_tiles):
    first, last = [], []
    for n in seq_tiles:
        first += [1] + [0] * (n - 1)
        last += [0] * (n - 1) + [1]
    return np.asarray(first, np.int32), np.asarray(last, np.int32)


def kernel(x_prompt, x_sample, meta_tokens, ffn1_norm, ffn1_w_gate_up, ffn1_w_down, mix_norm, w_in, ssd_conv_w,
           ssd_conv_b, ssd_dt_bias, ssd_a_log, ssd_d, ssd_norm, ssd_w_proj, hg_lb_table, hg_norm, hg_w_proj, w_out,
           ffn2_norm, ffn2_w_gate_up, ffn2_w_down, final_norm):
    groups = (x_prompt, x_sample)
    assert len(ffn1_norm) == 1, "single-layer block"
    pad_rows = TILE - N_META
    parts, seq_tiles = [], []
    for x in groups:
        b, s, dm = x.shape
        assert dm == D_MODEL and s % TILE == 0
        meta = jnp.broadcast_to(meta_tokens[None].astype(x.dtype), (b, N_META, dm))
        seq = jnp.concatenate([jnp.zeros((b, pad_rows, dm), x.dtype), meta, x], axis=1)
        parts.append(seq.reshape(b * (s + TILE), dm))
        seq_tiles += [s // TILE + 1] * b
    h0 = jnp.concatenate(parts, axis=0)
    assert h0.shape[0] % ROW_TILE == 0
    first, last = _tile_flags(seq_tiles)

    def ffn_weights(w_gate_up, w_down):
        return (w_gate_up[0, :, :D_FF].astype(BF16), w_gate_up[0, :, D_FF:].astype(BF16), w_down[0].astype(BF16))

    row = lambda v: v.reshape(1, -1).astype(F32)
    fnw = row(final_norm)

    h1 = _ffn(h0, row(ffn1_norm[0]), *ffn_weights(ffn1_w_gate_up, ffn1_w_down), fnw, final=False)

    pts = np.cumsum((0,) + IN_WIDTHS)
    z_w, xbc_w, dt_w, hq_w, hf_w, hi_w, hg_w, gate_w = (w_in[0][:, pts[j]:pts[j + 1]] for j in range(8))
    dt_w = jnp.pad(dt_w, ((0, 0), (0, LANES - 2 * SSD_HEADS)))
    w_packed = jnp.concatenate([z_w, xbc_w, hq_w, hf_w, hi_w, hg_w, gate_w, dt_w], axis=1).astype(BF16)
    lb = jnp.cumsum(jax.nn.softmax(hg_lb_table.astype(F32), axis=1), axis=1)[:, 0].reshape(1, 2 * D_MODEL)
    dtb = jnp.pad(ssd_dt_bias[0].astype(F32).reshape(1, -1), ((0, 0), (0, LANES - 2 * SSD_HEADS)))
    sz, xbc, q, k, lf, v, sg, gates, dt = _inproj(h1, row(mix_norm[0]), w_packed, lb, dtb)

    cw = jnp.pad(ssd_conv_w[0].astype(F32), ((0, 8 - SSD_CONV), (0, 0)))
    cb = row(ssd_conv_b[0])
    arow = jnp.pad(-jnp.exp(ssd_a_log[0].astype(F32)).reshape(1, -1), ((0, 0), (0, LANES - 2 * SSD_HEADS)))
    head_of_col = np.arange(D_MODEL) // SSD_HEADDIM
    expands = [jnp.asarray((np.arange(LANES)[:, None] == head_of_col[None, :] + SSD_HEADS * d), BF16)
               for d in range(2)]
    dexp = jnp.repeat(ssd_d[0].astype(F32), SSD_HEADDIM).reshape(1, -1)

    scan_in = (xbc, dt, q, k, v, lf, cw, cb, arow)
    y_f, o_f = _scan(first, last, *scan_in, expands[0], None, bwd=False)
    bra, brb = _scan(first, last, *scan_in, expands[1],
                     (y_f, o_f, sz, sg, dexp, row(ssd_norm[0]), row(hg_norm[0])), bwd=True)

    h2 = _merge(bra, brb, gates, h1, ssd_w_proj[0].astype(BF16), hg_w_proj[0].astype(BF16), w_out[0].astype(BF16))
    y = _ffn(h2, row(ffn2_norm[0]), *ffn_weights(ffn2_w_gate_up, ffn2_w_down), fnw, final=True)

    outs, off = [], 0
    for x in groups:
        b, s, dm = x.shape
        n = b * (s + TILE)
        outs.append(y[off:off + n].reshape(b, s + TILE, dm)[:, TILE:])
        off += n
    return tuple(outs)
```

```python
import functools

import numpy as np
import jax
import jax.numpy as jnp
from jax import lax
from jax.experimental import pallas as pl
from jax.experimental.pallas import tpu as pltpu

F32 = jnp.float32
BF16 = jnp.bfloat16

D_MODEL = 1024
N_META = 16
EPS = 1e-6
D_FF = 2816
SSD_HEADS = 16
SSD_HEADDIM = 64
SSD_GROUPS = 4
SSD_STATE = 128
SSD_CONV = 5
SSD_GN = SSD_GROUPS * SSD_STATE
SSD_CONV_DIM = D_MODEL + 2 * SSD_GN
HG_HEADS = 8
HG_HEADDIM = 128
HG_CHUNK = 64
IN_WIDTHS = (D_MODEL, SSD_CONV_DIM, 2 * SSD_HEADS, D_MODEL, 2 * D_MODEL, D_MODEL, D_MODEL, 2 * D_MODEL)

TILE = 128
ROW_TILE = 512
HALO_ROWS = 16
SUBLANES = 8
LANES = 128
VMEM_LIMIT = 56 * 1024 * 1024


def _rms(x, w):
    return x * lax.rsqrt(jnp.mean(x * x, axis=-1, keepdims=True) + EPS) * w


def _dot(a, b):
    return jnp.dot(a, b, preferred_element_type=F32)


def _dot_nt(a, b):
    return lax.dot_general(a, b, (((1,), (1,)), ((), ())), preferred_element_type=F32)


def _dot_tn(a, b):
    return lax.dot_general(a, b, (((0,), (0,)), ((), ())), preferred_element_type=F32)


def _split3(x):
    x1 = x.astype(BF16)
    r1 = x - x1.astype(F32)
    x2 = r1.astype(BF16)
    x3 = (r1 - x2.astype(F32)).astype(BF16)
    return x1, x2, x3


def _dot01_rhs3(m01, x):
    x1, x2, x3 = _split3(x)
    return _dot(m01, x1) + _dot(m01, x2) + _dot(m01, x3)


def _dot01_lhs3(x, m01):
    x1, x2, x3 = _split3(x)
    return _dot(x1, m01) + _dot(x2, m01) + _dot(x3, m01)


def _resident(shape):
    return pl.BlockSpec(shape, lambda i, *_: (0,) * len(shape), pipeline_mode=pl.Buffered(1))


def _params(semantics):
    return pltpu.CompilerParams(dimension_semantics=(semantics,), vmem_limit_bytes=VMEM_LIMIT)


def _ffn_compute(x, nw_ref, wgu_ref, wd_ref, n_chunks):
    hn = _rms(x, nw_ref[...]).astype(BF16)
    tf = D_FF // n_chunks
    acc = jnp.zeros(x.shape, F32)
    for c in range(n_chunks):
        g = _dot(hn, wgu_ref[:, c * tf:(c + 1) * tf])
        u = _dot(hn, wgu_ref[:, D_FF + c * tf:D_FF + (c + 1) * tf])
        a = (jax.nn.silu(g) * u).astype(BF16)
        acc = acc + _dot(a, wd_ref[c * tf:(c + 1) * tf, :])
    return x + 0.5 * acc


def _ffn_in_body(xp_ref, xs_ref, tail_ref, nw_ref, wgu_ref, wd_ref, o_ref, *, steps_p, steps_s):
    i = pl.program_id(0)
    x = jnp.where(i < steps_p, xp_ref[...], jnp.where(i < steps_p + steps_s, xs_ref[...], tail_ref[...]))
    o_ref[...] = _ffn_compute(x, nw_ref, wgu_ref, wd_ref, 2)


def _ffn_in(xp, xs, tail, nw, wgu, wd):
    steps_p, steps_s = xp.shape[0] // ROW_TILE, xs.shape[0] // ROW_TILE
    steps = steps_p + steps_s + 1
    blk = (ROW_TILE, D_MODEL)
    return pl.pallas_call(
        functools.partial(_ffn_in_body, steps_p=steps_p, steps_s=steps_s),
        grid=(steps,),
        in_specs=[pl.BlockSpec(blk, lambda i: (jnp.minimum(i, steps_p - 1), 0)),
                  pl.BlockSpec(blk, lambda i: (jnp.clip(i - steps_p, 0, steps_s - 1), 0)),
                  pl.BlockSpec(blk, lambda i: (0, 0)),
                  _resident((1, D_MODEL)), _resident((D_MODEL, 2 * D_FF)), _resident((D_FF, D_MODEL))],
        out_specs=pl.BlockSpec(blk, lambda i: (i, 0)),
        out_shape=jax.ShapeDtypeStruct((steps * ROW_TILE, D_MODEL), F32),
        compiler_params=_params("parallel"),
        name="ffn_in",
    )(xp, xs, tail, nw, wgu, wd)


def _ffn_out_body(x_ref, nw_ref, wgu_ref, wd_ref, fnw_ref, o_ref):
    o_ref[...] = _rms(_ffn_compute(x_ref[...], nw_ref, wgu_ref, wd_ref, 2), fnw_ref[...])


def _ffn_out(h, row0, rows, nw, wgu, wd, fnw):
    blk = (ROW_TILE, D_MODEL)
    off = row0 // ROW_TILE
    return pl.pallas_call(
        _ffn_out_body,
        grid=(rows // ROW_TILE,),
        in_specs=[pl.BlockSpec(blk, lambda i: (i + off, 0)),
                  _resident((1, D_MODEL)), _resident((D_MODEL, 2 * D_FF)), _resident((D_FF, D_MODEL)),
                  _resident((1, D_MODEL))],
        out_specs=pl.BlockSpec(blk, lambda i: (i, 0)),
        out_shape=jax.ShapeDtypeStruct((rows, D_MODEL), F32),
        compiler_params=_params("parallel"),
        name="ffn_out",
    )(h, nw, wgu, wd, fnw)


_C_Z, _C_XBC, _C_Q, _C_F, _C_V, _C_G, _C_GATE, _C_DT, _C_END = (
    0, 1024, 3072, 4096, 6144, 7168, 8192, 10240, 10368)


def _inproj_body(h_ref, nw_ref, w_ref, lb_ref, dtb_ref,
                 sz_ref, xbc_ref, q_ref, k_ref, lf_ref, v_ref, sg_ref, gate_ref, dt_ref):
    hn = _rms(h_ref[...], nw_ref[...]).astype(BF16)

    def proj(lo, hi):
        return _dot(hn, w_ref[:, lo:hi])

    sz_ref[...] = jax.nn.silu(proj(_C_Z, _C_XBC)).astype(BF16)
    xbc_ref[...] = proj(_C_XBC, _C_Q).astype(BF16)
    q_ref[...] = jax.nn.silu(proj(_C_Q, _C_F)).astype(BF16)
    lb = lb_ref[...]
    f = lb + (1.0 - lb) * jax.nn.sigmoid(proj(_C_F, _C_V))
    k_ref[...] = (1.0 - f).astype(BF16)
    lf_ref[...] = jnp.log(f)
    v_ref[...] = proj(_C_V, _C_G).astype(BF16)
    sg_ref[...] = jax.nn.silu(proj(_C_G, _C_GATE)).astype(BF16)
    gate_ref[...] = jax.nn.sigmoid(proj(_C_GATE, _C_DT)).astype(BF16)
    dt_ref[...] = jax.nn.softplus(proj(_C_DT, _C_END) + dtb_ref[...])


def _inproj(h, nw, w, lb, dtb):
    t = h.shape[0]

    def row(width):
        return pl.BlockSpec((ROW_TILE, width), lambda i: (i, 0))

    widths = (D_MODEL, SSD_CONV_DIM, D_MODEL, 2 * D_MODEL, 2 * D_MODEL, D_MODEL, D_MODEL, 2 * D_MODEL, LANES)
    dtypes = (BF16, BF16, BF16, BF16, F32, BF16, BF16, BF16, F32)
    return pl.pallas_call(
        _inproj_body,
        grid=(t // ROW_TILE,),
        in_specs=[row(D_MODEL), _resident((1, D_MODEL)), _resident((D_MODEL, _C_END)),
                  _resident((1, 2 * D_MODEL)), _resident((1, LANES))],
        out_specs=[row(wd) for wd in widths],
        out_shape=[jax.ShapeDtypeStruct((t, wd), dt) for wd, dt in zip(widths, dtypes)],
        compiler_params=_params("parallel"),
        name="inproj",
    )(h, nw, w, lb, dtb)


def _conv_silu(x, prev8, next8, cw_ref, cb_ref):
    half = SSD_CONV // 2
    sub = lax.broadcasted_iota(jnp.int32, (SUBLANES, 1), 0)
    acc = cb_ref[...] + cw_ref[half:half + 1, :] * x
    for j in range(SSD_CONV):
        sh = half - j
        if sh == 0:
            continue
        rolled = pltpu.roll(x, sh % TILE, axis=0)
        if sh > 0:
            fix = jnp.where(sub < sh, pltpu.roll(prev8, sh, axis=0), rolled[:SUBLANES])
            rolled = jnp.concatenate([fix, rolled[SUBLANES:]], axis=0)
        else:
            fix = jnp.where(sub >= SUBLANES + sh, pltpu.roll(next8, SUBLANES + sh, axis=0), rolled[-SUBLANES:])
            rolled = jnp.concatenate([rolled[:-SUBLANES], fix], axis=0)
        acc = acc + cw_ref[j:j + 1, :] * rolled
    return jax.nn.silu(acc)


def _ssd_tile(xs, bm, cm, dtv, arow_ref, e_ref, s_ssd, *, bwd):
    r_i = lax.broadcasted_iota(jnp.int32, (TILE, TILE), 0)
    c_i = lax.broadcasted_iota(jnp.int32, (TILE, TILE), 1)
    causal = (r_i <= c_i) if bwd else (r_i >= c_i)
    tri = jnp.where(causal, 1.0, 0.0).astype(BF16)
    end_row = 0 if bwd else TILE - 1
    acs = _dot01_rhs3(tri, dtv * arow_ref[...])
    total = acs[end_row:end_row + 1, :]
    e_in = jnp.exp(acs)
    e_end = jnp.exp(total - acs)
    expand = e_ref[...]
    dt_x = _dot(dtv.astype(BF16), expand)
    w_x = _dot((dtv * e_end).astype(BF16), expand)
    ein_x = _dot(e_in.astype(BF16), expand)
    etot_x = _dot01_lhs3(jnp.broadcast_to(jnp.exp(total), (SUBLANES, LANES)), expand)[0:1, :]
    xdt = (xs * dt_x).astype(BF16)
    xdtd = (xs * w_x).astype(BF16)
    acs_t = acs.T

    lane_i = lax.broadcasted_iota(jnp.int32, (TILE, LANES), 1)
    lo_half = lane_i < SSD_HEADDIM
    s_old = s_ssd[...]
    s_old_b = s_old.astype(BF16)
    neg_inf = jnp.float32(-jnp.inf)
    hpg = SSD_HEADS // SSD_GROUPS
    gw = hpg * SSD_HEADDIM
    y_parts = []
    for g in range(SSD_GROUPS):
        bg = bm[:, g * SSD_STATE:(g + 1) * SSD_STATE]
        cg = cm[:, g * SSD_STATE:(g + 1) * SSD_STATE]
        cb = _dot_nt(cg, bg)
        for pair in range(hpg // 2):
            ms = []
            for hh in range(2):
                lane = (SSD_HEADS if bwd else 0) + g * hpg + pair * 2 + hh
                seg = acs[:, lane:lane + 1] - acs_t[lane:lane + 1, :]
                ms.append((cb * jnp.exp(jnp.where(causal, seg, neg_inf))).astype(BF16))
            c0 = g * gw + pair * LANES
            xp = xdt[:, c0:c0 + LANES]
            zero = jnp.zeros_like(xp)
            rhs = jnp.concatenate([jnp.where(lo_half, xp, zero), jnp.where(lo_half, zero, xp)], axis=0)
            y_parts.append(_dot(jnp.concatenate(ms, axis=1), rhs))
    y_diag = jnp.concatenate(y_parts, axis=1)
    y_off_parts = []
    for g in range(SSD_GROUPS):
        cols = slice(g * gw, (g + 1) * gw)
        bg = bm[:, g * SSD_STATE:(g + 1) * SSD_STATE]
        cg = cm[:, g * SSD_STATE:(g + 1) * SSD_STATE]
        y_off_parts.append(_dot(cg, s_old_b[:, cols]))
        s_ssd[:, cols] = s_old[:, cols] * etot_x[:, cols] + _dot_tn(bg, xdtd[:, cols])
    return y_diag + jnp.concatenate(y_off_parts, axis=1) * ein_x


def _hgrn_tile(q_all, k_all, v_all, lf_all, s_hg, *, bwd):
    cq = HG_CHUNK
    r64 = lax.broadcasted_iota(jnp.int32, (cq, cq), 0)
    c64 = lax.broadcasted_iota(jnp.int32, (cq, cq), 1)
    causal64 = (r64 <= c64) if bwd else (r64 >= c64)
    tri64 = jnp.where(causal64, 1.0, 0.0).astype(BF16)
    ref_row = (cq - 1 - cq // 2) if bwd else cq // 2
    end64 = 0 if bwd else cq - 1
    o_chunks = [None] * (TILE // cq)
    for step in range(TILE // cq):
        ci = (TILE // cq - 1 - step) if bwd else step
        rows = slice(ci * cq, (ci + 1) * cq)
        qc, kc, vc = q_all[rows], k_all[rows], v_all[rows]
        bcs = _dot01_rhs3(tri64, lf_all[rows])
        bref = bcs[ref_row:ref_row + 1, :]
        btot = bcs[end64:end64 + 1, :]
        qe = (qc * jnp.exp(bcs - bref)).astype(BF16)
        ke = (kc * jnp.exp(bref - bcs)).astype(BF16)
        kd = (kc * jnp.exp(btot - bcs)).astype(BF16)
        qd = (qc * jnp.exp(bcs)).astype(BF16)
        st = s_hg[...]
        st_b = st.astype(BF16)
        decay = jnp.exp(btot)
        o_parts = []
        for h in range(HG_HEADS):
            cols = slice(h * HG_HEADDIM, (h + 1) * HG_HEADDIM)
            a = jnp.where(causal64, _dot_nt(qe[:, cols], ke[:, cols]), 0.0).astype(BF16)
            o_parts.append(_dot(a, vc[:, cols]) + _dot_nt(qd[:, cols], st_b[:, cols]))
            s_hg[:, cols] = st[:, cols] * decay[:, cols] + _dot_tn(vc[:, cols], kd[:, cols])
        o_chunks[ci] = jnp.concatenate(o_parts, axis=1)
    return jnp.concatenate(o_chunks, axis=0)


def _scan_fwd_body(tile_ref, out_ref, prev_ref, next_ref, first_ref, last_ref,
                   xprev_ref, xbc_ref, xnext_ref, dt_ref, q_ref, k_ref, v_ref, lf_ref,
                   cw_ref, cb_ref, arow_ref, e_ref,
                   act_ref, y_ref, o_ref, s_ssd, s_hg):
    i = pl.program_id(0)
    first = first_ref[i] == 1
    last = last_ref[i] == 1

    @pl.when(first)
    def _():
        s_ssd[...] = jnp.zeros_like(s_ssd)
        s_hg[...] = jnp.zeros_like(s_hg)

    prev8 = xprev_ref[...].astype(F32)[HALO_ROWS - SUBLANES:] * jnp.where(first, 0.0, 1.0)
    next8 = xnext_ref[...].astype(F32)[:SUBLANES] * jnp.where(last, 0.0, 1.0)
    act = _conv_silu(xbc_ref[...].astype(F32), prev8, next8, cw_ref, cb_ref)
    act_b = act.astype(BF16)
    act_ref[...] = act_b

    row = lax.broadcasted_iota(jnp.int32, (TILE, 1), 0)
    valid = jnp.where(jnp.logical_and(first, row < TILE - N_META), 0.0, 1.0)
    y_ref[...] = _ssd_tile(act[:, :D_MODEL], act_b[:, D_MODEL:D_MODEL + SSD_GN], act_b[:, D_MODEL + SSD_GN:],
                           dt_ref[...] * valid, arow_ref, e_ref, s_ssd, bwd=False)
    o_ref[...] = _hgrn_tile(q_ref[...].astype(F32), k_ref[...].astype(F32), v_ref[...], lf_ref[...], s_hg,
                            bwd=False)


def _scan_bwd_body(tile_ref, reset_ref,
                   act_ref, dt_ref, q_ref, k_ref, v_ref, lf_ref, arow_ref, e_ref,
                   yf_ref, of_ref, sz_ref, sg_ref, dexp_ref, snorm_ref, hnorm_ref,
                   bra_ref, brb_ref, s_ssd, s_hg):
    i = pl.program_id(0)

    @pl.when(reset_ref[i] == 1)
    def _():
        s_ssd[...] = jnp.zeros_like(s_ssd)
        s_hg[...] = jnp.zeros_like(s_hg)

    xs = act_ref[:, :D_MODEL].astype(F32)
    y_dir = _ssd_tile(xs, act_ref[:, D_MODEL:D_MODEL + SSD_GN], act_ref[:, D_MODEL + SSD_GN:],
                      dt_ref[...], arow_ref, e_ref, s_ssd, bwd=True)
    o_dir = _hgrn_tile(q_ref[...].astype(F32), k_ref[...].astype(F32), v_ref[...], lf_ref[...], s_hg, bwd=True)

    y = (yf_ref[...] + y_dir + dexp_ref[...] * xs) * sz_ref[...].astype(F32)
    bra_ref[...] = _rms(y, snorm_ref[...]).astype(BF16)
    o = of_ref[...] + o_dir
    o_parts = []
    for h in range(HG_HEADS):
        oh = o[:, h * HG_HEADDIM:(h + 1) * HG_HEADDIM]
        o_parts.append(oh * lax.rsqrt(jnp.mean(oh * oh, axis=-1, keepdims=True) + EPS))
    o = jnp.concatenate(o_parts, axis=1) * hnorm_ref[...] * sg_ref[...].astype(F32)
    brb_ref[...] = o.astype(BF16)


def _scan_scratch():
    return [pltpu.VMEM((SSD_STATE, D_MODEL), F32), pltpu.VMEM((HG_HEADDIM, D_MODEL), F32)]


def _scan_fwd(tables, n_seq, xbc, dt, q, k, v, lf, cw, cb, arow, expand):
    t = xbc.shape[0]

    def rows(width, col=0):
        return pl.BlockSpec((TILE, width), lambda i, tile, *_: (tile[i], col))

    def outs(width):
        return pl.BlockSpec((TILE, width), lambda i, tile, out, *_: (out[i], 0))

    def const(shape):
        return pl.BlockSpec(shape, lambda i, *_: (0,) * len(shape))

    halo = (HALO_ROWS, SSD_CONV_DIM)
    in_specs = [
        pl.BlockSpec(halo, lambda i, tile, out, prev, nxt, *_: (prev[i], 0)),
        rows(SSD_CONV_DIM),
        pl.BlockSpec(halo, lambda i, tile, out, prev, nxt, *_: (nxt[i], 0)),
        rows(LANES), rows(D_MODEL), rows(D_MODEL, 0), rows(D_MODEL), rows(D_MODEL, 0),
        const((SUBLANES, SSD_CONV_DIM)), const((1, SSD_CONV_DIM)), const((1, LANES)), const((LANES, D_MODEL)),
    ]
    grid_spec = pltpu.PrefetchScalarGridSpec(
        num_scalar_prefetch=len(tables), grid=(len(tables[0]),), in_specs=in_specs,
        out_specs=[outs(SSD_CONV_DIM), outs(D_MODEL), outs(D_MODEL)], scratch_shapes=_scan_scratch())
    t_out = t + n_seq * TILE
    return pl.pallas_call(
        _scan_fwd_body, grid_spec=grid_spec,
        out_shape=[jax.ShapeDtypeStruct((t_out, SSD_CONV_DIM), BF16), jax.ShapeDtypeStruct((t_out, D_MODEL), F32),
                   jax.ShapeDtypeStruct((t_out, D_MODEL), F32)],
        compiler_params=_params("arbitrary"), name="scan_fwd",
    )(*tables, xbc, xbc, xbc, dt, q, k, v, lf, cw, cb, arow, expand)


def _scan_bwd(tables, act, dt, q, k, v, lf, arow, expand, y_f, o_f, sz, sg, dexp, snorm, hnorm):
    t = act.shape[0]

    def rows(width, col=0):
        return pl.BlockSpec((TILE, width), lambda i, tile, *_: (tile[i], col))

    def const(shape):
        return pl.BlockSpec(shape, lambda i, *_: (0,) * len(shape))

    in_specs = [rows(SSD_CONV_DIM), rows(LANES), rows(D_MODEL), rows(D_MODEL, 1), rows(D_MODEL), rows(D_MODEL, 1),
                const((1, LANES)), const((LANES, D_MODEL)),
                rows(D_MODEL), rows(D_MODEL), rows(D_MODEL), rows(D_MODEL),
                const((1, D_MODEL)), const((1, D_MODEL)), const((1, D_MODEL))]
    grid_spec = pltpu.PrefetchScalarGridSpec(
        num_scalar_prefetch=len(tables), grid=(len(tables[0]),), in_specs=in_specs,
        out_specs=[rows(D_MODEL), rows(D_MODEL)], scratch_shapes=_scan_scratch())
    return pl.pallas_call(
        _scan_bwd_body, grid_spec=grid_spec,
        out_shape=[jax.ShapeDtypeStruct((t, D_MODEL), BF16)] * 2,
        compiler_params=_params("arbitrary"), name="scan_bwd",
    )(*tables, act, dt, q, k, v, lf, arow, expand, y_f, o_f, sz, sg, dexp, snorm, hnorm)


def _merge_body(bra_ref, brb_ref, gate_ref, h_ref, wa_ref, wb_ref, wo_ref, o_ref):
    a = _dot(bra_ref[...], wa_ref[...])
    b = _dot(brb_ref[...], wb_ref[...])
    gates = gate_ref[...].astype(F32)
    merged = gates[:, :D_MODEL] * a + gates[:, D_MODEL:] * b
    o_ref[...] = h_ref[...] + _dot(merged.astype(BF16), wo_ref[...])


def _merge(rows, bra, brb, gates, h, wa, wb, wo):
    def row(width):
        return pl.BlockSpec((ROW_TILE, width), lambda i: (i, 0))

    sq = _resident((D_MODEL, D_MODEL))
    return pl.pallas_call(
        _merge_body,
        grid=(rows // ROW_TILE,),
        in_specs=[row(D_MODEL), row(D_MODEL), row(2 * D_MODEL), row(D_MODEL), sq, sq, sq],
        out_specs=row(D_MODEL),
        out_shape=jax.ShapeDtypeStruct((rows, D_MODEL), F32),
        compiler_params=_params("parallel"),
        name="merge",
    )(bra, brb, gates, h, wa, wb, wo)


def _scan_tables(seq_rows, meta_row, n_tiles):
    per_tile = TILE // HALO_ROWS
    meta_tile = meta_row // TILE
    f_tile, f_out, f_prev, f_next, f_first, f_last, b_tile, b_reset = [], [], [], [], [], [], [], []
    row0 = 0
    for seq, rows in enumerate(seq_rows):
        t0, n = row0 // TILE, rows // TILE
        f_tile += [meta_tile] + [t0 + j for j in range(n)]
        f_out += [n_tiles + seq] + [t0 + j for j in range(n)]
        f_prev += [0, (meta_tile + 1) * per_tile - 1] + [(t0 + j) * per_tile - 1 for j in range(1, n)]
        f_next += [t0 * per_tile] + [(t0 + j + 1) * per_tile for j in range(n - 1)] + [0]
        f_first += [1] + [0] * n
        f_last += [0] * n + [1]
        b_tile += [t0 + j for j in reversed(range(n))]
        b_reset += [1] + [0] * (n - 1)
        row0 += rows
    as_i32 = lambda xs: tuple(np.asarray(x, np.int32) for x in xs)
    return as_i32((f_tile, f_out, f_prev, f_next, f_first, f_last)), as_i32((b_tile, b_reset))


def kernel(x_prompt, x_sample, meta_tokens, ffn1_norm, ffn1_w_gate_up, ffn1_w_down, mix_norm, w_in, ssd_conv_w,
           ssd_conv_b, ssd_dt_bias, ssd_a_log, ssd_d, ssd_norm, ssd_w_proj, hg_lb_table, hg_norm, hg_w_proj, w_out,
           ffn2_norm, ffn2_w_gate_up, ffn2_w_down, final_norm):
    assert len(ffn1_norm) == 1, "single-layer block"
    groups = (x_prompt, x_sample)
    seq_rows = []
    for x in groups:
        b, s, dm = x.shape
        assert dm == D_MODEL and s % TILE == 0 and (b * s) % ROW_TILE == 0
        seq_rows += [s] * b
    n_rows = sum(seq_rows)
    xp, xs = (x.reshape(-1, D_MODEL) for x in groups)
    tail = jnp.pad(meta_tokens.astype(F32), ((TILE - N_META, ROW_TILE - TILE), (0, 0)))
    fwd_tables, bwd_tables = _scan_tables(seq_rows, n_rows, (n_rows + ROW_TILE) // TILE)

    row = lambda v: v.reshape(1, -1).astype(F32)

    h1 = _ffn_in(xp, xs, tail, row(ffn1_norm[0]), ffn1_w_gate_up[0].astype(BF16), ffn1_w_down[0].astype(BF16))

    pts = np.cumsum((0,) + IN_WIDTHS)
    w_b = w_in[0].astype(BF16)
    z_w, xbc_w, dt_w, hq_w, hf_w, hi_w, hg_w, gate_w = (w_b[:, pts[j]:pts[j + 1]] for j in range(8))
    dt_w = jnp.pad(dt_w, ((0, 0), (0, LANES - 2 * SSD_HEADS)))
    w_packed = jnp.concatenate([z_w, xbc_w, hq_w, hf_w, hi_w, hg_w, gate_w, dt_w], axis=1)
    lb = jnp.cumsum(jax.nn.softmax(hg_lb_table.astype(F32), axis=1), axis=1)[:, 0].reshape(1, 2 * D_MODEL)
    dtb = jnp.pad(ssd_dt_bias[0].astype(F32).reshape(1, -1), ((0, 0), (0, LANES - 2 * SSD_HEADS)))
    sz, xbc, q, k, lf, v, sg, gates, dt = _inproj(h1, row(mix_norm[0]), w_packed, lb, dtb)

    cw = jnp.pad(ssd_conv_w[0].astype(F32), ((0, SUBLANES - SSD_CONV), (0, 0)))
    cb = row(ssd_conv_b[0])
    arow = jnp.pad(-jnp.exp(ssd_a_log[0].astype(F32)).reshape(1, -1), ((0, 0), (0, LANES - 2 * SSD_HEADS)))
    head_of_col = np.arange(D_MODEL) // SSD_HEADDIM
    expands = [jnp.asarray((np.arange(LANES)[:, None] == head_of_col[None, :] + SSD_HEADS * d), BF16)
               for d in range(2)]
    dexp = jnp.repeat(ssd_d[0].astype(F32), SSD_HEADDIM).reshape(1, -1)

    act, y_f, o_f = _scan_fwd(fwd_tables, len(seq_rows), xbc, dt, q, k, v, lf, cw, cb, arow, expands[0])
    bra, brb = _scan_bwd(bwd_tables, act, dt, q, k, v, lf, arow, expands[1], y_f, o_f, sz, sg, dexp,
                         row(ssd_norm[0]), row(hg_norm[0]))

    h2 = _merge(n_rows, bra, brb, gates, h1, ssd_w_proj[0].astype(BF16), hg_w_proj[0].astype(BF16),
                w_out[0].astype(BF16))

    wgu2, wd2 = ffn2_w_gate_up[0].astype(BF16), ffn2_w_down[0].astype(BF16)
    outs, row0 = [], 0
    for x in groups:
        b, s, dm = x.shape
        y = _ffn_out(h2, row0, b * s, row(ffn2_norm[0]), wgu2, wd2, row(final_norm))
        outs.append(y.reshape(b, s, dm))
        row0 += b * s
    return tuple(outs)
```

```python
import functools

import numpy as np
import jax
import jax.numpy as jnp
from jax import lax
from jax.experimental import pallas as pl
from jax.experimental.pallas import tpu as pltpu

F32 = jnp.float32
BF16 = jnp.bfloat16

D_MODEL = 1024
N_META = 16
EPS = 1e-6
D_FF = 2816
SSD_HEADS = 16
SSD_HEADDIM = 64
SSD_GROUPS = 4
SSD_STATE = 128
SSD_CONV = 5
SSD_GN = SSD_GROUPS * SSD_STATE
SSD_CONV_DIM = D_MODEL + 2 * SSD_GN
HG_HEADS = 8
HG_HEADDIM = 128
HG_CHUNK = 64
IN_WIDTHS = (D_MODEL, SSD_CONV_DIM, 2 * SSD_HEADS, D_MODEL, 2 * D_MODEL, D_MODEL, D_MODEL, 2 * D_MODEL)

TILE = 128
ROW_TILE = 512
MERGE_TILE = 256
FFN_CHUNKS = 11
HALO_ROWS = 16
SUBLANES = 8
LANES = 128
VMEM_LIMIT = 56 * 1024 * 1024


def _rms(x, w):
    return x * lax.rsqrt(jnp.mean(x * x, axis=-1, keepdims=True) + EPS) * w


def _dot(a, b):
    return jnp.dot(a, b, preferred_element_type=F32)


def _dot_nt(a, b):
    return lax.dot_general(a, b, (((1,), (1,)), ((), ())), preferred_element_type=F32)


def _dot_tn(a, b):
    return lax.dot_general(a, b, (((0,), (0,)), ((), ())), preferred_element_type=F32)


def _split3(x):
    x1 = x.astype(BF16)
    r1 = x - x1.astype(F32)
    x2 = r1.astype(BF16)
    x3 = (r1 - x2.astype(F32)).astype(BF16)
    return x1, x2, x3


def _dot01_rhs3(m01, x):
    x1, x2, x3 = _split3(x)
    return _dot(m01, x1) + _dot(m01, x2) + _dot(m01, x3)


def _dot01_lhs3(x, m01):
    x1, x2, x3 = _split3(x)
    return _dot(x1, m01) + _dot(x2, m01) + _dot(x3, m01)


def _resident(shape):
    return pl.BlockSpec(shape, lambda i, *_: (0,) * len(shape), pipeline_mode=pl.Buffered(1))


def _params(semantics):
    return pltpu.CompilerParams(dimension_semantics=(semantics,), vmem_limit_bytes=VMEM_LIMIT)


def _ffn_compute(x, nw_ref, wgu_ref, wd_ref, n_chunks):
    hn = _rms(x, nw_ref[...]).astype(BF16)
    tf = D_FF // n_chunks
    acc = jnp.zeros(x.shape, F32)
    for c in range(n_chunks):
        g = _dot(hn, wgu_ref[:, c * tf:(c + 1) * tf])
        u = _dot(hn, wgu_ref[:, D_FF + c * tf:D_FF + (c + 1) * tf])
        a = (jax.nn.silu(g) * u).astype(BF16)
        acc = acc + _dot(a, wd_ref[c * tf:(c + 1) * tf, :])
    return x + 0.5 * acc


def _ffn_in_body(xp_ref, xs_ref, tail_ref, nw_ref, wgu_ref, wd_ref, o_ref, *, steps_p, steps_s):
    i = pl.program_id(0)
    x = jnp.where(i < steps_p, xp_ref[...], jnp.where(i < steps_p + steps_s, xs_ref[...], tail_ref[...]))
    o_ref[...] = _ffn_compute(x, nw_ref, wgu_ref, wd_ref, FFN_CHUNKS)


def _ffn_in(xp, xs, tail, nw, wgu, wd):
    steps_p, steps_s = xp.shape[0] // ROW_TILE, xs.shape[0] // ROW_TILE
    steps = steps_p + steps_s + 1
    blk = (ROW_TILE, D_MODEL)
    return pl.pallas_call(
        functools.partial(_ffn_in_body, steps_p=steps_p, steps_s=steps_s),
        grid=(steps,),
        in_specs=[pl.BlockSpec(blk, lambda i: (jnp.minimum(i, steps_p - 1), 0)),
                  pl.BlockSpec(blk, lambda i: (jnp.clip(i - steps_p, 0, steps_s - 1), 0)),
                  pl.BlockSpec(blk, lambda i: (0, 0)),
                  _resident((1, D_MODEL)), _resident((D_MODEL, 2 * D_FF)), _resident((D_FF, D_MODEL))],
        out_specs=pl.BlockSpec(blk, lambda i: (i, 0)),
        out_shape=jax.ShapeDtypeStruct((steps * ROW_TILE, D_MODEL), F32),
        compiler_params=_params("parallel"),
        name="ffn_in",
    )(xp, xs, tail, nw, wgu, wd)


def _ffn_out_body(x_ref, nw_ref, wgu_ref, wd_ref, fnw_ref, o_ref):
    o_ref[...] = _rms(_ffn_compute(x_ref[...], nw_ref, wgu_ref, wd_ref, FFN_CHUNKS), fnw_ref[...])


def _ffn_out(h, row0, rows, nw, wgu, wd, fnw):
    blk = (ROW_TILE, D_MODEL)
    off = row0 // ROW_TILE
    return pl.pallas_call(
        _ffn_out_body,
        grid=(rows // ROW_TILE,),
        in_specs=[pl.BlockSpec(blk, lambda i: (i + off, 0)),
                  _resident((1, D_MODEL)), _resident((D_MODEL, 2 * D_FF)), _resident((D_FF, D_MODEL)),
                  _resident((1, D_MODEL))],
        out_specs=pl.BlockSpec(blk, lambda i: (i, 0)),
        out_shape=jax.ShapeDtypeStruct((rows, D_MODEL), F32),
        compiler_params=_params("parallel"),
        name="ffn_out",
    )(h, nw, wgu, wd, fnw)


_C_Z, _C_XBC, _C_Q, _C_F, _C_V, _C_G, _C_GATE, _C_DT, _C_END = (
    0, 1024, 3072, 4096, 6144, 7168, 8192, 10240, 10368)


def _inproj_body(h_ref, nw_ref, w_ref, lb_ref, dtb_ref, szsg_ref, xbc_ref, qkv_ref, lf3_ref, gate_ref, dt_ref):
    hn = _rms(h_ref[...], nw_ref[...]).astype(BF16)
    dm = D_MODEL

    def proj(lo, hi):
        return _dot(hn, w_ref[:, lo:hi])

    szsg_ref[:, :dm] = jax.nn.silu(proj(_C_Z, _C_XBC)).astype(BF16)
    xbc_ref[...] = proj(_C_XBC, _C_Q).astype(BF16)
    qkv_ref[:, :dm] = jax.nn.silu(proj(_C_Q, _C_F)).astype(BF16)
    for d in range(2):
        lb = lb_ref[:, d * dm:(d + 1) * dm]
        f = lb + (1.0 - lb) * jax.nn.sigmoid(proj(_C_F + d * dm, _C_F + (d + 1) * dm))
        qkv_ref[:, (1 + d) * dm:(2 + d) * dm] = (1.0 - f).astype(BF16)
        for j, piece in enumerate(_split3(jnp.log(f))):
            lf3_ref[:, (3 * d + j) * dm:(3 * d + j + 1) * dm] = piece
    qkv_ref[:, 3 * dm:] = proj(_C_V, _C_G).astype(BF16)
    szsg_ref[:, dm:] = jax.nn.silu(proj(_C_G, _C_GATE)).astype(BF16)
    gate_ref[...] = jax.nn.sigmoid(proj(_C_GATE, _C_DT)).astype(BF16)
    dt_ref[...] = jax.nn.softplus(proj(_C_DT, _C_END) + dtb_ref[...])


def _inproj(h, nw, w, lb, dtb):
    t = h.shape[0]

    def row(width):
        return pl.BlockSpec((MERGE_TILE, width), lambda i: (i, 0))

    widths = (2 * D_MODEL, SSD_CONV_DIM, 4 * D_MODEL, 6 * D_MODEL, 2 * D_MODEL, LANES)
    dtypes = (BF16, BF16, BF16, BF16, BF16, F32)
    return pl.pallas_call(
        _inproj_body,
        grid=(t // MERGE_TILE,),
        in_specs=[row(D_MODEL), _resident((1, D_MODEL)), _resident((D_MODEL, _C_END)),
                  _resident((1, 2 * D_MODEL)), _resident((1, LANES))],
        out_specs=[row(wd) for wd in widths],
        out_shape=[jax.ShapeDtypeStruct((t, wd), dt) for wd, dt in zip(widths, dtypes)],
        compiler_params=_params("parallel"),
        name="inproj",
    )(h, nw, w, lb, dtb)


def _conv_silu(x, prev8, next8, cw_ref, cb_ref):
    half = SSD_CONV // 2
    sub = lax.broadcasted_iota(jnp.int32, (SUBLANES, 1), 0)
    acc = cb_ref[...] + cw_ref[half:half + 1, :] * x
    for j in range(SSD_CONV):
        sh = half - j
        if sh == 0:
            continue
        rolled = pltpu.roll(x, sh % TILE, axis=0)
        if sh > 0:
            fix = jnp.where(sub < sh, pltpu.roll(prev8, sh, axis=0), rolled[:SUBLANES])
            rolled = jnp.concatenate([fix, rolled[SUBLANES:]], axis=0)
        else:
            fix = jnp.where(sub >= SUBLANES + sh, pltpu.roll(next8, SUBLANES + sh, axis=0), rolled[-SUBLANES:])
            rolled = jnp.concatenate([rolled[:-SUBLANES], fix], axis=0)
        acc = acc + cw_ref[j:j + 1, :] * rolled
    return jax.nn.silu(acc)


def _mix_tile(xs, bm, cm, dtv, qkv_ref, lf3_ref, arow_ref, e_ref, s_ssd, s_hg, *, bwd):
    hpg = SSD_HEADS // SSD_GROUPS
    gw = hpg * SSD_HEADDIM
    cq = HG_CHUNK
    n_chunks = TILE // cq
    chunk_order = list(reversed(range(n_chunks))) if bwd else list(range(n_chunks))
    g_cols = [slice(g * gw, (g + 1) * gw) for g in range(SSD_GROUPS)]
    g_state = [slice(g * SSD_STATE, (g + 1) * SSD_STATE) for g in range(SSD_GROUPS)]
    h_cols = [slice(h * HG_HEADDIM, (h + 1) * HG_HEADDIM) for h in range(HG_HEADS)]

    r_i = lax.broadcasted_iota(jnp.int32, (TILE, TILE), 0)
    c_i = lax.broadcasted_iota(jnp.int32, (TILE, TILE), 1)
    causal = (r_i <= c_i) if bwd else (r_i >= c_i)
    tri = jnp.where(causal, 1.0, 0.0).astype(BF16)
    causal64 = causal[:cq, :cq]
    tri_chunks = jnp.where(jnp.logical_and(causal, r_i // cq == c_i // cq), 1.0, 0.0).astype(BF16)
    dm = D_MODEL
    k0 = (2 if bwd else 1) * dm
    q_all = qkv_ref[:, :dm].astype(F32)
    k_all = qkv_ref[:, k0:k0 + dm].astype(F32)
    v_all = qkv_ref[:, 3 * dm:]
    end_row = 0 if bwd else TILE - 1
    ref_row = (cq - 1 - cq // 2) if bwd else cq // 2
    end64 = 0 if bwd else cq - 1

    s_old = s_ssd[...]
    s_old_b = s_old.astype(BF16)
    cbs = [_dot_nt(cm[:, g_state[g]], bm[:, g_state[g]]) for g in range(SSD_GROUPS)]
    y_offs = [_dot(cm[:, g_state[g]], s_old_b[:, g_cols[g]]) for g in range(SSD_GROUPS)]
    acs = _dot01_rhs3(tri, dtv * arow_ref[...])
    bcs_tile = (_dot(tri_chunks, lf3_ref[:, :dm]) + _dot(tri_chunks, lf3_ref[:, dm:2 * dm])
                + _dot(tri_chunks, lf3_ref[:, 2 * dm:]))
    bcs_all = [bcs_tile[ci * cq:(ci + 1) * cq] for ci in range(n_chunks)]

    total = acs[end_row:end_row + 1, :]
    e_in = jnp.exp(acs)
    e_end = jnp.exp(total - acs)
    expand = e_ref[...]
    dt_x = _dot(dtv.astype(BF16), expand)
    w_x = _dot((dtv * e_end).astype(BF16), expand)
    ein_x = _dot(e_in.astype(BF16), expand)
    etot_x = _dot01_lhs3(jnp.broadcast_to(jnp.exp(total), (SUBLANES, LANES)), expand)[0:1, :]
    acs_t = acs.T

    hg = []
    for ci in range(n_chunks):
        rows = slice(ci * cq, (ci + 1) * cq)
        qc, kc, bcs = q_all[rows], k_all[rows], bcs_all[ci]
        bref = bcs[ref_row:ref_row + 1, :]
        btot = bcs[end64:end64 + 1, :]
        qe = (qc * jnp.exp(bcs - bref)).astype(BF16)
        ke = (kc * jnp.exp(bref - bcs)).astype(BF16)
        kd = (kc * jnp.exp(btot - bcs)).astype(BF16)
        qd = (qc * jnp.exp(bcs)).astype(BF16)
        scores = [_dot_nt(qe[:, c], ke[:, c]) for c in h_cols]
        hg.append((scores, kd, qd, jnp.exp(btot), v_all[rows]))

    xdt = (xs * dt_x).astype(BF16)
    xdtd = (xs * w_x).astype(BF16)
    for g in range(SSD_GROUPS):
        s_ssd[:, g_cols[g]] = (s_old[:, g_cols[g]] * etot_x[:, g_cols[g]]
                               + _dot_tn(bm[:, g_state[g]], xdtd[:, g_cols[g]]))

    def hgrn_chunk(ci, st):
        scores, kd, qd, decay, vc = hg[ci]
        st_b = st.astype(BF16)
        inter = [_dot_nt(qd[:, c], st_b[:, c]) for c in h_cols]
        upd = [_dot_tn(vc[:, c], kd[:, c]) for c in h_cols]
        intra = [_dot(jnp.where(causal64, scores[h], 0.0).astype(BF16), vc[:, h_cols[h]]) for h in range(HG_HEADS)]
        o = jnp.concatenate([a + b for a, b in zip(intra, inter)], axis=1)
        return o, st * decay + jnp.concatenate(upd, axis=1)

    o_chunks = [None] * n_chunks
    o_chunks[chunk_order[0]], st = hgrn_chunk(chunk_order[0], s_hg[...])

    lane_i = lax.broadcasted_iota(jnp.int32, (TILE, LANES), 1)
    lo_half = lane_i < SSD_HEADDIM
    neg_inf = jnp.float32(-jnp.inf)
    y_parts = []
    for g in range(SSD_GROUPS):
        for pair in range(hpg // 2):
            ms = []
            for hh in range(2):
                lane = (SSD_HEADS if bwd else 0) + g * hpg + pair * 2 + hh
                seg = acs[:, lane:lane + 1] - acs_t[lane:lane + 1, :]
                ms.append((cbs[g] * jnp.exp(jnp.where(causal, seg, neg_inf))).astype(BF16))
            c0 = g * gw + pair * LANES
            xp = xdt[:, c0:c0 + LANES]
            zero = jnp.zeros_like(xp)
            rhs = jnp.concatenate([jnp.where(lo_half, xp, zero), jnp.where(lo_half, zero, xp)], axis=0)
            y_parts.append(_dot(jnp.concatenate(ms, axis=1), rhs))
    y = jnp.concatenate(y_parts, axis=1) + jnp.concatenate(y_offs, axis=1) * ein_x

    for ci in chunk_order[1:]:
        o_chunks[ci], st = hgrn_chunk(ci, st)
    s_hg[...] = st
    return y, jnp.concatenate(o_chunks, axis=0)


def _scan_fwd_body(tile_ref, out_ref, prev_ref, next_ref, first_ref, last_ref,
                   xprev_ref, xbc_ref, xnext_ref, dt_ref, qkv_ref, lf3_ref,
                   cw_ref, cb_ref, arow_ref, e_ref,
                   act_ref, yo_ref, s_ssd, s_hg):
    i = pl.program_id(0)
    first = first_ref[i] == 1
    last = last_ref[i] == 1

    @pl.when(first)
    def _():
        s_ssd[...] = jnp.zeros_like(s_ssd)
        s_hg[...] = jnp.zeros_like(s_hg)

    prev8 = xprev_ref[...].astype(F32)[HALO_ROWS - SUBLANES:] * jnp.where(first, 0.0, 1.0)
    next8 = xnext_ref[...].astype(F32)[:SUBLANES] * jnp.where(last, 0.0, 1.0)
    act = _conv_silu(xbc_ref[...].astype(F32), prev8, next8, cw_ref, cb_ref)
    act_b = act.astype(BF16)
    act_ref[...] = act_b

    row = lax.broadcasted_iota(jnp.int32, (TILE, 1), 0)
    valid = jnp.where(jnp.logical_and(first, row < TILE - N_META), 0.0, 1.0)
    y, o = _mix_tile(act[:, :D_MODEL], act_b[:, D_MODEL:D_MODEL + SSD_GN], act_b[:, D_MODEL + SSD_GN:],
                     dt_ref[...] * valid, qkv_ref, lf3_ref, arow_ref, e_ref, s_ssd, s_hg, bwd=False)
    yo_ref[:, :D_MODEL] = y
    yo_ref[:, D_MODEL:] = o


def _scan_bwd_body(tile_ref, reset_ref, act_ref, dt_ref, qkv_ref, lf3_ref, arow_ref, e_ref, yo_ref, s_ssd, s_hg):
    i = pl.program_id(0)

    @pl.when(reset_ref[i] == 1)
    def _():
        s_ssd[...] = jnp.zeros_like(s_ssd)
        s_hg[...] = jnp.zeros_like(s_hg)

    y, o = _mix_tile(act_ref[:, :D_MODEL].astype(F32), act_ref[:, D_MODEL:D_MODEL + SSD_GN],
                     act_ref[:, D_MODEL + SSD_GN:], dt_ref[...], qkv_ref, lf3_ref, arow_ref, e_ref, s_ssd, s_hg,
                     bwd=True)
    yo_ref[:, :D_MODEL] = y
    yo_ref[:, D_MODEL:] = o


def _scan_scratch():
    return [pltpu.VMEM((SSD_STATE, D_MODEL), F32), pltpu.VMEM((HG_HEADDIM, D_MODEL), F32)]


def _scan_fwd(tables, n_seq, xbc, dt, qkv, lf3, cw, cb, arow, expand):
    t = xbc.shape[0]

    def rows(width, col=0):
        return pl.BlockSpec((TILE, width), lambda i, tile, *_: (tile[i], col))

    def outs(width):
        return pl.BlockSpec((TILE, width), lambda i, tile, out, *_: (out[i], 0))

    def const(shape):
        return pl.BlockSpec(shape, lambda i, *_: (0,) * len(shape))

    halo = (HALO_ROWS, SSD_CONV_DIM)
    in_specs = [
        pl.BlockSpec(halo, lambda i, tile, out, prev, nxt, *_: (prev[i], 0)),
        rows(SSD_CONV_DIM),
        pl.BlockSpec(halo, lambda i, tile, out, prev, nxt, *_: (nxt[i], 0)),
        rows(LANES), rows(4 * D_MODEL), rows(3 * D_MODEL, 0),
        const((SUBLANES, SSD_CONV_DIM)), const((1, SSD_CONV_DIM)), const((1, LANES)), const((LANES, D_MODEL)),
    ]
    grid_spec = pltpu.PrefetchScalarGridSpec(
        num_scalar_prefetch=len(tables), grid=(len(tables[0]),), in_specs=in_specs,
        out_specs=[outs(SSD_CONV_DIM), outs(2 * D_MODEL)], scratch_shapes=_scan_scratch())
    t_out = t + n_seq * TILE
    return pl.pallas_call(
        _scan_fwd_body, grid_spec=grid_spec,
        out_shape=[jax.ShapeDtypeStruct((t_out, SSD_CONV_DIM), BF16),
                   jax.ShapeDtypeStruct((t_out, 2 * D_MODEL), F32)],
        compiler_params=_params("arbitrary"), name="scan_fwd",
    )(*tables, xbc, xbc, xbc, dt, qkv, lf3, cw, cb, arow, expand)


def _scan_bwd(tables, act, dt, qkv, lf3, arow, expand):
    t = act.shape[0]

    def rows(width, col=0):
        return pl.BlockSpec((TILE, width), lambda i, tile, *_: (tile[i], col))

    def const(shape):
        return pl.BlockSpec(shape, lambda i, *_: (0,) * len(shape))

    in_specs = [rows(SSD_CONV_DIM), rows(LANES), rows(4 * D_MODEL), rows(3 * D_MODEL, 1),
                const((1, LANES)), const((LANES, D_MODEL))]
    grid_spec = pltpu.PrefetchScalarGridSpec(
        num_scalar_prefetch=len(tables), grid=(len(tables[0]),), in_specs=in_specs,
        out_specs=rows(2 * D_MODEL), scratch_shapes=_scan_scratch())
    return pl.pallas_call(
        _scan_bwd_body, grid_spec=grid_spec,
        out_shape=jax.ShapeDtypeStruct((t, 2 * D_MODEL), F32),
        compiler_params=_params("arbitrary"), name="scan_bwd",
    )(*tables, act, dt, qkv, lf3, arow, expand)


def _merge_body(yof_ref, yob_ref, xs_ref, szsg_ref, gate_ref, h_ref, dexp_ref, snorm_ref, hnorm_ref,
                wa_ref, wb_ref, wo_ref, o_ref):
    dm = D_MODEL
    y = yof_ref[:, :dm] + yob_ref[:, :dm] + dexp_ref[...] * xs_ref[...].astype(F32)
    y = y * szsg_ref[:, :dm].astype(F32)
    bra = _rms(y, snorm_ref[...]).astype(BF16)
    o_parts = []
    for h in range(HG_HEADS):
        cols = slice(dm + h * HG_HEADDIM, dm + (h + 1) * HG_HEADDIM)
        oh = yof_ref[:, cols] + yob_ref[:, cols]
        o_parts.append(oh * lax.rsqrt(jnp.mean(oh * oh, axis=-1, keepdims=True) + EPS))
    brb = (jnp.concatenate(o_parts, axis=1) * hnorm_ref[...] * szsg_ref[:, dm:].astype(F32)).astype(BF16)
    a = _dot(bra, wa_ref[...])
    b = _dot(brb, wb_ref[...])
    merged = gate_ref[:, :dm].astype(F32) * a + gate_ref[:, dm:].astype(F32) * b
    o_ref[...] = h_ref[...] + _dot(merged.astype(BF16), wo_ref[...])


def _merge(rows, yo_f, yo_b, act, szsg, gates, h, dexp, snorm, hnorm, wa, wb, wo):
    def row(width):
        return pl.BlockSpec((MERGE_TILE, width), lambda i: (i, 0))

    sq = _resident((D_MODEL, D_MODEL))
    vec = _resident((1, D_MODEL))
    return pl.pallas_call(
        _merge_body,
        grid=(rows // MERGE_TILE,),
        in_specs=[row(2 * D_MODEL), row(2 * D_MODEL), row(D_MODEL), row(2 * D_MODEL), row(2 * D_MODEL),
                  row(D_MODEL), vec, vec, vec, sq, sq, sq],
        out_specs=row(D_MODEL),
        out_shape=jax.ShapeDtypeStruct((rows, D_MODEL), F32),
        compiler_params=_params("parallel"),
        name="merge",
    )(yo_f, yo_b, act, szsg, gates, h, dexp, snorm, hnorm, wa, wb, wo)


def _scan_tables(seq_rows, meta_row, n_tiles):
    per_tile = TILE // HALO_ROWS
    meta_tile = meta_row // TILE
    f_tile, f_out, f_prev, f_next, f_first, f_last, b_tile, b_reset = [], [], [], [], [], [], [], []
    row0 = 0
    for seq, rows in enumerate(seq_rows):
        t0, n = row0 // TILE, rows // TILE
        f_tile += [meta_tile] + [t0 + j for j in range(n)]
        f_out += [n_tiles + seq] + [t0 + j for j in range(n)]
        f_prev += [0, (meta_tile + 1) * per_tile - 1] + [(t0 + j) * per_tile - 1 for j in range(1, n)]
        f_next += [t0 * per_tile] + [(t0 + j + 1) * per_tile for j in range(n - 1)] + [0]
        f_first += [1] + [0] * n
        f_last += [0] * n + [1]
        b_tile += [t0 + j for j in reversed(range(n))]
        b_reset += [1] + [0] * (n - 1)
        row0 += rows
    as_i32 = lambda xs: tuple(np.asarray(x, np.int32) for x in xs)
    return as_i32((f_tile, f_out, f_prev, f_next, f_first, f_last)), as_i32((b_tile, b_reset))


def kernel(x_prompt, x_sample, meta_tokens, ffn1_norm, ffn1_w_gate_up, ffn1_w_down, mix_norm, w_in, ssd_conv_w,
           ssd_conv_b, ssd_dt_bias, ssd_a_log, ssd_d, ssd_norm, ssd_w_proj, hg_lb_table, hg_norm, hg_w_proj, w_out,
           ffn2_norm, ffn2_w_gate_up, ffn2_w_down, final_norm):
    assert len(ffn1_norm) == 1, "single-layer block"
    groups = (x_prompt, x_sample)
    seq_rows = []
    for x in groups:
        b, s, dm = x.shape
        assert dm == D_MODEL and s % TILE == 0 and (b * s) % ROW_TILE == 0
        seq_rows += [s] * b
    n_rows = sum(seq_rows)
    xp, xs = (x.reshape(-1, D_MODEL) for x in groups)
    tail = jnp.pad(meta_tokens.astype(F32), ((TILE - N_META, ROW_TILE - TILE), (0, 0)))
    fwd_tables, bwd_tables = _scan_tables(seq_rows, n_rows, (n_rows + ROW_TILE) // TILE)

    row = lambda v: v.reshape(1, -1).astype(F32)

    h1 = _ffn_in(xp, xs, tail, row(ffn1_norm[0]), ffn1_w_gate_up[0].astype(BF16), ffn1_w_down[0].astype(BF16))

    pts = np.cumsum((0,) + IN_WIDTHS)
    w_b = w_in[0].astype(BF16)
    z_w, xbc_w, dt_w, hq_w, hf_w, hi_w, hg_w, gate_w = (w_b[:, pts[j]:pts[j + 1]] for j in range(8))
    dt_w = jnp.pad(dt_w, ((0, 0), (0, LANES - 2 * SSD_HEADS)))
    w_packed = jnp.concatenate([z_w, xbc_w, hq_w, hf_w, hi_w, hg_w, gate_w, dt_w], axis=1)
    lb = jnp.cumsum(jax.nn.softmax(hg_lb_table.astype(F32), axis=1), axis=1)[:, 0].reshape(1, 2 * D_MODEL)
    dtb = jnp.pad(ssd_dt_bias[0].astype(F32).reshape(1, -1), ((0, 0), (0, LANES - 2 * SSD_HEADS)))
    szsg, xbc, qkv, lf3, gates, dt = _inproj(h1, row(mix_norm[0]), w_packed, lb, dtb)

    cw = jnp.pad(ssd_conv_w[0].astype(F32), ((0, SUBLANES - SSD_CONV), (0, 0)))
    cb = row(ssd_conv_b[0])
    arow = jnp.pad(-jnp.exp(ssd_a_log[0].astype(F32)).reshape(1, -1), ((0, 0), (0, LANES - 2 * SSD_HEADS)))
    head_of_col = np.arange(D_MODEL) // SSD_HEADDIM
    expands = [jnp.asarray((np.arange(LANES)[:, None] == head_of_col[None, :] + SSD_HEADS * d), BF16)
               for d in range(2)]
    dexp = jnp.repeat(ssd_d[0].astype(F32), SSD_HEADDIM).reshape(1, -1)

    act, yo_f = _scan_fwd(fwd_tables, len(seq_rows), xbc, dt, qkv, lf3, cw, cb, arow, expands[0])
    yo_b = _scan_bwd(bwd_tables, act, dt, qkv, lf3, arow, expands[1])

    h2 = _merge(n_rows, yo_f, yo_b, act, szsg, gates, h1, dexp, row(ssd_norm[0]), row(hg_norm[0]),
                ssd_w_proj[0].astype(BF16), hg_w_proj[0].astype(BF16), w_out[0].astype(BF16))

    wgu2, wd2 = ffn2_w_gate_up[0].astype(BF16), ffn2_w_down[0].astype(BF16)
    outs, row0 = [], 0
    for x in groups:
        b, s, dm = x.shape
        y = _ffn_out(h2, row0, b * s, row(ffn2_norm[0]), wgu2, wd2, row(final_norm))
        outs.append(y.reshape(b, s, dm))
        row0 += b * s
    return tuple(outs)
```

```python
import functools

import numpy as np
import jax
import jax.numpy as jnp
from jax import lax
from jax.experimental import pallas as pl
from jax.experimental.pallas import tpu as pltpu

F32 = jnp.float32
BF16 = jnp.bfloat16

D_MODEL = 1024
N_META = 16
EPS = 1e-6
D_FF = 2816
SSD_HEADS = 16
SSD_HEADDIM = 64
SSD_GROUPS = 4
SSD_STATE = 128
SSD_CONV = 5
SSD_GN = SSD_GROUPS * SSD_STATE
SSD_CONV_DIM = D_MODEL + 2 * SSD_GN
HG_HEADS = 8
HG_HEADDIM = 128
HG_CHUNK = 64
IN_WIDTHS = (D_MODEL, SSD_CONV_DIM, 2 * SSD_HEADS, D_MODEL, 2 * D_MODEL, D_MODEL, D_MODEL, 2 * D_MODEL)

TILE = 128
ROW_TILE = 512
MERGE_TILE = 256
FFN_CHUNKS = 11
PROJ_CHUNK = 256
HALO_ROWS = 16
SUBLANES = 8
LANES = 128
VMEM_LIMIT = 56 * 1024 * 1024


def _rms(x, w):
    return x * lax.rsqrt(jnp.mean(x * x, axis=-1, keepdims=True) + EPS) * w


def _dot(a, b):
    return jnp.dot(a, b, preferred_element_type=F32)


def _dot_nt(a, b):
    return lax.dot_general(a, b, (((1,), (1,)), ((), ())), preferred_element_type=F32)


def _dot_tn(a, b):
    return lax.dot_general(a, b, (((0,), (0,)), ((), ())), preferred_element_type=F32)


def _split3(x):
    x1 = x.astype(BF16)
    r1 = x - x1.astype(F32)
    x2 = r1.astype(BF16)
    x3 = (r1 - x2.astype(F32)).astype(BF16)
    return x1, x2, x3


def _dot01_rhs3(m01, x):
    x1, x2, x3 = _split3(x)
    return _dot(m01, x1) + _dot(m01, x2) + _dot(m01, x3)


def _dot01_lhs3(x, m01):
    x1, x2, x3 = _split3(x)
    return _dot(x1, m01) + _dot(x2, m01) + _dot(x3, m01)


def _resident(shape):
    return pl.BlockSpec(shape, lambda i, *_: (0,) * len(shape), pipeline_mode=pl.Buffered(1))


def _params(semantics):
    return pltpu.CompilerParams(dimension_semantics=(semantics,), vmem_limit_bytes=VMEM_LIMIT)


def _ffn_compute(x, nw_ref, wgu_ref, wd_ref, n_chunks):
    hn = _rms(x, nw_ref[...]).astype(BF16)
    tf = D_FF // n_chunks
    acc = jnp.zeros(x.shape, F32)
    for c in range(n_chunks):
        g = _dot(hn, wgu_ref[:, c * tf:(c + 1) * tf])
        u = _dot(hn, wgu_ref[:, D_FF + c * tf:D_FF + (c + 1) * tf])
        a = (jax.nn.silu(g) * u).astype(BF16)
        acc = acc + _dot(a, wd_ref[c * tf:(c + 1) * tf, :])
    return x + 0.5 * acc


def _ffn_in_body(xp_ref, xs_ref, tail_ref, nw_ref, wgu_ref, wd_ref, o_ref, *, steps_p, steps_s):
    i = pl.program_id(0)
    x = jnp.where(i < steps_p, xp_ref[...], jnp.where(i < steps_p + steps_s, xs_ref[...], tail_ref[...]))
    o_ref[...] = _ffn_compute(x, nw_ref, wgu_ref, wd_ref, FFN_CHUNKS)


def _ffn_in(xp, xs, tail, nw, wgu, wd):
    steps_p, steps_s = xp.shape[0] // ROW_TILE, xs.shape[0] // ROW_TILE
    steps = steps_p + steps_s + 1
    blk = (ROW_TILE, D_MODEL)
    return pl.pallas_call(
        functools.partial(_ffn_in_body, steps_p=steps_p, steps_s=steps_s),
        grid=(steps,),
        in_specs=[pl.BlockSpec(blk, lambda i: (jnp.minimum(i, steps_p - 1), 0)),
                  pl.BlockSpec(blk, lambda i: (jnp.clip(i - steps_p, 0, steps_s - 1), 0)),
                  pl.BlockSpec(blk, lambda i: (0, 0)),
                  _resident((1, D_MODEL)), _resident((D_MODEL, 2 * D_FF)), _resident((D_FF, D_MODEL))],
        out_specs=pl.BlockSpec(blk, lambda i: (i, 0)),
        out_shape=jax.ShapeDtypeStruct((steps * ROW_TILE, D_MODEL), F32),
        compiler_params=_params("parallel"),
        name="ffn_in",
    )(xp, xs, tail, nw, wgu, wd)


def _ffn_out_body(x_ref, nw_ref, wgu_ref, wd_ref, fnw_ref, o_ref):
    o_ref[...] = _rms(_ffn_compute(x_ref[...], nw_ref, wgu_ref, wd_ref, FFN_CHUNKS), fnw_ref[...])


def _ffn_out(h, row0, rows, nw, wgu, wd, fnw):
    blk = (ROW_TILE, D_MODEL)
    off = row0 // ROW_TILE
    return pl.pallas_call(
        _ffn_out_body,
        grid=(rows // ROW_TILE,),
        in_specs=[pl.BlockSpec(blk, lambda i: (i + off, 0)),
                  _resident((1, D_MODEL)), _resident((D_MODEL, 2 * D_FF)), _resident((D_FF, D_MODEL)),
                  _resident((1, D_MODEL))],
        out_specs=pl.BlockSpec(blk, lambda i: (i, 0)),
        out_shape=jax.ShapeDtypeStruct((rows, D_MODEL), F32),
        compiler_params=_params("parallel"),
        name="ffn_out",
    )(h, nw, wgu, wd, fnw)


_C_Z, _C_XBC, _C_Q, _C_F, _C_V, _C_G, _C_GATE, _C_DT, _C_END = (
    0, 1024, 3072, 4096, 6144, 7168, 8192, 10240, 10368)


def _inproj_body(prev_tbl, next_tbl, keep_prev_tbl, keep_next_tbl,
                 h_ref, hprev_ref, hnext_ref, nw_ref, w_ref, lb_ref, dtb_ref, cw_ref, cb_ref,
                 szsg_ref, act_ref, xhead_ref, xmeta_ref, qkv_ref, lf3_ref, gate_ref, dt_ref):
    i = pl.program_id(0)
    rows = h_ref.shape[0]
    h_ext = jnp.concatenate([h_ref[...], hprev_ref[...], hnext_ref[...]], axis=0)
    hn_ext = _rms(h_ext, nw_ref[...]).astype(BF16)
    hn = hn_ext[:rows]
    dm = D_MODEL

    cw = PROJ_CHUNK
    keep_prev = jnp.where(keep_prev_tbl[i] == 1, 1.0, 0.0)
    keep_next = jnp.where(keep_next_tbl[i] == 1, 1.0, 0.0)

    def proj(w0, c, lhs=hn):
        return _dot(lhs, w_ref[:, w0 + c:w0 + c + cw])

    def light_chunks():
        for c in range(0, dm, cw):
            szsg_ref[:, c:c + cw] = jax.nn.silu(proj(_C_Z, c)).astype(BF16)
            yield
        for c in range(0, dm, cw):
            qkv_ref[:, c:c + cw] = jax.nn.silu(proj(_C_Q, c)).astype(BF16)
            yield
        for d in range(2):
            for c in range(0, dm, cw):
                lb = lb_ref[:, d * dm + c:d * dm + c + cw]
                f = lb + (1.0 - lb) * jax.nn.sigmoid(proj(_C_F + d * dm, c))
                qkv_ref[:, (1 + d) * dm + c:(1 + d) * dm + c + cw] = (1.0 - f).astype(BF16)
                for j, piece in enumerate(_split3(jnp.log(f))):
                    lf3_ref[:, (3 * d + j) * dm + c:(3 * d + j) * dm + c + cw] = piece
                yield
        for c in range(0, dm, cw):
            qkv_ref[:, 3 * dm + c:3 * dm + c + cw] = proj(_C_V, c).astype(BF16)
            yield
        for c in range(0, dm, cw):
            szsg_ref[:, dm + c:dm + c + cw] = jax.nn.silu(proj(_C_G, c)).astype(BF16)
            yield
        for c in range(0, 2 * dm, cw):
            gate_ref[:, c:c + cw] = jax.nn.sigmoid(proj(_C_GATE, c)).astype(BF16)
            yield

    light = light_chunks()
    n_conv = SSD_CONV_DIM // cw
    n_light = (_C_DT - SSD_CONV_DIM) // cw
    for k in range(n_conv):
        c = k * cw
        xe = proj(_C_XBC, c, hn_ext)
        for _ in range(n_light // n_conv):
            next(light)
        x = xe[:rows]
        act = _conv_silu(x, xe[rows:rows + SUBLANES] * keep_prev, xe[rows + SUBLANES:] * keep_next,
                         cw_ref[:, c:c + cw], cb_ref[:, c:c + cw])
        act_ref[:, c:c + cw] = act.astype(BF16)
        xhead_ref[:, c:c + cw] = x[:HALO_ROWS].astype(BF16)
        xmeta_ref[:, c:c + cw] = x[TILE - N_META:TILE].astype(BF16)
    for _ in light:
        pass
    dt_ref[...] = jax.nn.softplus(_dot(hn, w_ref[:, _C_DT:_C_END]) + dtb_ref[...])


def _inproj(tables, h, nw, w, lb, dtb, cw, cb):
    t = h.shape[0]
    steps = t // MERGE_TILE

    def row(width):
        return pl.BlockSpec((MERGE_TILE, width), lambda i, *_: (i, 0))

    def head(width):
        return pl.BlockSpec((HALO_ROWS, width), lambda i, *_: (i, 0))

    halo = (SUBLANES, D_MODEL)
    in_specs = [row(D_MODEL),
                pl.BlockSpec(halo, lambda i, prev, nxt, *_: (prev[i], 0)),
                pl.BlockSpec(halo, lambda i, prev, nxt, *_: (nxt[i], 0)),
                _resident((1, D_MODEL)), _resident((D_MODEL, _C_END)), _resident((1, 2 * D_MODEL)),
                _resident((1, LANES)), _resident((SUBLANES, SSD_CONV_DIM)), _resident((1, SSD_CONV_DIM))]
    out_specs = [row(2 * D_MODEL), row(SSD_CONV_DIM), head(SSD_CONV_DIM), head(SSD_CONV_DIM),
                 row(4 * D_MODEL), row(6 * D_MODEL), row(2 * D_MODEL), row(LANES)]
    out_shape = [jax.ShapeDtypeStruct((t, 2 * D_MODEL), BF16), jax.ShapeDtypeStruct((t, SSD_CONV_DIM), BF16),
                 jax.ShapeDtypeStruct((steps * HALO_ROWS, SSD_CONV_DIM), BF16),
                 jax.ShapeDtypeStruct((steps * HALO_ROWS, SSD_CONV_DIM), BF16),
                 jax.ShapeDtypeStruct((t, 4 * D_MODEL), BF16), jax.ShapeDtypeStruct((t, 6 * D_MODEL), BF16),
                 jax.ShapeDtypeStruct((t, 2 * D_MODEL), BF16), jax.ShapeDtypeStruct((t, LANES), F32)]
    grid_spec = pltpu.PrefetchScalarGridSpec(
        num_scalar_prefetch=len(tables), grid=(steps,), in_specs=in_specs, out_specs=out_specs)
    return pl.pallas_call(
        _inproj_body, grid_spec=grid_spec, out_shape=out_shape,
        compiler_params=_params("parallel"), name="inproj",
    )(*tables, h, h, h, nw, w, lb, dtb, cw, cb)


def _conv_silu(x, prev8, next8, cw, cb):
    half = SSD_CONV // 2
    sub = lax.broadcasted_iota(jnp.int32, (SUBLANES, 1), 0)
    acc = cb + cw[half:half + 1, :] * x
    for j in range(SSD_CONV):
        sh = half - j
        if sh == 0:
            continue
        rolled = pltpu.roll(x, sh % x.shape[0], axis=0)
        if sh > 0:
            fix = jnp.where(sub < sh, pltpu.roll(prev8, sh, axis=0), rolled[:SUBLANES])
            rolled = jnp.concatenate([fix, rolled[SUBLANES:]], axis=0)
        else:
            fix = jnp.where(sub >= SUBLANES + sh, pltpu.roll(next8, SUBLANES + sh, axis=0), rolled[-SUBLANES:])
            rolled = jnp.concatenate([rolled[:-SUBLANES], fix], axis=0)
        acc = acc + cw[j:j + 1, :] * rolled
    return jax.nn.silu(acc)


def _mix_tile(xs, bm, cm, dtv, qkv_ref, lf3_ref, arow_ref, e_ref, s_ssd, s_hg, *, bwd):
    hpg = SSD_HEADS // SSD_GROUPS
    gw = hpg * SSD_HEADDIM
    cq = HG_CHUNK
    n_chunks = TILE // cq
    chunk_order = list(reversed(range(n_chunks))) if bwd else list(range(n_chunks))
    g_cols = [slice(g * gw, (g + 1) * gw) for g in range(SSD_GROUPS)]
    g_state = [slice(g * SSD_STATE, (g + 1) * SSD_STATE) for g in range(SSD_GROUPS)]
    h_cols = [slice(h * HG_HEADDIM, (h + 1) * HG_HEADDIM) for h in range(HG_HEADS)]

    r_i = lax.broadcasted_iota(jnp.int32, (TILE, TILE), 0)
    c_i = lax.broadcasted_iota(jnp.int32, (TILE, TILE), 1)
    causal = (r_i <= c_i) if bwd else (r_i >= c_i)
    tri = jnp.where(causal, 1.0, 0.0).astype(BF16)
    causal64 = causal[:cq, :cq]
    tri_chunks = jnp.where(jnp.logical_and(causal, r_i // cq == c_i // cq), 1.0, 0.0).astype(BF16)
    dm = D_MODEL
    k0 = (2 if bwd else 1) * dm
    q_all = qkv_ref[:, :dm].astype(F32)
    k_all = qkv_ref[:, k0:k0 + dm].astype(F32)
    v_all = qkv_ref[:, 3 * dm:]
    end_row = 0 if bwd else TILE - 1
    ref_row = (cq - 1 - cq // 2) if bwd else cq // 2
    end64 = 0 if bwd else cq - 1

    s_old = s_ssd[...]
    s_old_b = s_old.astype(BF16)
    cbs = [_dot_nt(cm[:, g_state[g]], bm[:, g_state[g]]) for g in range(SSD_GROUPS)]
    y_offs = [_dot(cm[:, g_state[g]], s_old_b[:, g_cols[g]]) for g in range(SSD_GROUPS)]
    acs = _dot01_rhs3(tri, dtv * arow_ref[...])
    bcs_tile = (_dot(tri_chunks, lf3_ref[:, :dm]) + _dot(tri_chunks, lf3_ref[:, dm:2 * dm])
                + _dot(tri_chunks, lf3_ref[:, 2 * dm:]))
    bcs_all = [bcs_tile[ci * cq:(ci + 1) * cq] for ci in range(n_chunks)]

    total = acs[end_row:end_row + 1, :]
    e_in = jnp.exp(acs)
    e_end = jnp.exp(total - acs)
    expand = e_ref[...]
    dt_x = _dot(dtv.astype(BF16), expand)
    w_x = _dot((dtv * e_end).astype(BF16), expand)
    ein_x = _dot(e_in.astype(BF16), expand)
    etot_x = _dot01_lhs3(jnp.broadcast_to(jnp.exp(total), (SUBLANES, LANES)), expand)[0:1, :]
    acs_t = acs.T

    hg = []
    for ci in range(n_chunks):
        rows = slice(ci * cq, (ci + 1) * cq)
        qc, kc, bcs = q_all[rows], k_all[rows], bcs_all[ci]
        bref = bcs[ref_row:ref_row + 1, :]
        btot = bcs[end64:end64 + 1, :]
        qe = (qc * jnp.exp(bcs - bref)).astype(BF16)
        ke = (kc * jnp.exp(bref - bcs)).astype(BF16)
        kd = (kc * jnp.exp(btot - bcs)).astype(BF16)
        qd = (qc * jnp.exp(bcs)).astype(BF16)
        scores = [_dot_nt(qe[:, c], ke[:, c]) for c in h_cols]
        hg.append((scores, kd, qd, jnp.exp(btot), v_all[rows]))

    xdt = (xs * dt_x).astype(BF16)
    xdtd = (xs * w_x).astype(BF16)
    for g in range(SSD_GROUPS):
        s_ssd[:, g_cols[g]] = (s_old[:, g_cols[g]] * etot_x[:, g_cols[g]]
                               + _dot_tn(bm[:, g_state[g]], xdtd[:, g_cols[g]]))

    def hgrn_chunk(ci, st):
        scores, kd, qd, decay, vc = hg[ci]
        st_b = st.astype(BF16)
        inter = [_dot_nt(qd[:, c], st_b[:, c]) for c in h_cols]
        upd = [_dot_tn(vc[:, c], kd[:, c]) for c in h_cols]
        intra = [_dot(jnp.where(causal64, scores[h], 0.0).astype(BF16), vc[:, h_cols[h]]) for h in range(HG_HEADS)]
        o = jnp.concatenate([a + b for a, b in zip(intra, inter)], axis=1)
        return o, st * decay + jnp.concatenate(upd, axis=1)

    o_chunks = [None] * n_chunks
    o_chunks[chunk_order[0]], st = hgrn_chunk(chunk_order[0], s_hg[...])

    lane_i = lax.broadcasted_iota(jnp.int32, (TILE, LANES), 1)
    lo_half = lane_i < SSD_HEADDIM
    neg_inf = jnp.float32(-jnp.inf)
    y_parts = []
    for g in range(SSD_GROUPS):
        for pair in range(hpg // 2):
            ms = []
            for hh in range(2):
                lane = (SSD_HEADS if bwd else 0) + g * hpg + pair * 2 + hh
                seg = acs[:, lane:lane + 1] - acs_t[lane:lane + 1, :]
                ms.append((cbs[g] * jnp.exp(jnp.where(causal, seg, neg_inf))).astype(BF16))
            c0 = g * gw + pair * LANES
            xp = xdt[:, c0:c0 + LANES]
            zero = jnp.zeros_like(xp)
            rhs = jnp.concatenate([jnp.where(lo_half, xp, zero), jnp.where(lo_half, zero, xp)], axis=0)
            y_parts.append(_dot(jnp.concatenate(ms, axis=1), rhs))
    y = jnp.concatenate(y_parts, axis=1) + jnp.concatenate(y_offs, axis=1) * ein_x

    for ci in chunk_order[1:]:
        o_chunks[ci], st = hgrn_chunk(ci, st)
    s_hg[...] = st
    return y, jnp.concatenate(o_chunks, axis=0)


def _scan_fwd_body(tile_ref, out_ref, head_ref, first_ref,
                   act_ref, xmeta_ref, xhead_ref, dt_ref, qkv_ref, lf3_ref,
                   cw_ref, cb_ref, arow_ref, e_ref,
                   yo_ref, s_ssd, s_hg, meta_act):
    i = pl.program_id(0)
    first = first_ref[i] == 1

    @pl.when(first)
    def _():
        s_ssd[...] = jnp.zeros_like(s_ssd)
        s_hg[...] = jnp.zeros_like(s_hg)
        meta_act[...] = _conv_silu(xmeta_ref[...].astype(F32), jnp.zeros((SUBLANES, SSD_CONV_DIM), F32),
                                   xhead_ref[...].astype(F32)[:SUBLANES], cw_ref[...], cb_ref[...]).astype(BF16)

    n_pad = TILE - N_META
    act = jnp.concatenate([act_ref[:n_pad, :], jnp.where(first, meta_act[...], act_ref[n_pad:, :])], axis=0)
    row = lax.broadcasted_iota(jnp.int32, (TILE, 1), 0)
    valid = jnp.where(jnp.logical_and(first, row < n_pad), 0.0, 1.0)
    y, o = _mix_tile(act[:, :D_MODEL].astype(F32), act[:, D_MODEL:D_MODEL + SSD_GN], act[:, D_MODEL + SSD_GN:],
                     dt_ref[...] * valid, qkv_ref, lf3_ref, arow_ref, e_ref, s_ssd, s_hg, bwd=False)
    yo_ref[:, :D_MODEL] = y.astype(BF16)
    yo_ref[:, D_MODEL:] = o.astype(BF16)


def _scan_bwd_body(tile_ref, reset_ref, act_ref, dt_ref, qkv_ref, lf3_ref, arow_ref, e_ref, yo_ref, s_ssd, s_hg):
    i = pl.program_id(0)

    @pl.when(reset_ref[i] == 1)
    def _():
        s_ssd[...] = jnp.zeros_like(s_ssd)
        s_hg[...] = jnp.zeros_like(s_hg)

    y, o = _mix_tile(act_ref[:, :D_MODEL].astype(F32), act_ref[:, D_MODEL:D_MODEL + SSD_GN],
                     act_ref[:, D_MODEL + SSD_GN:], dt_ref[...], qkv_ref, lf3_ref, arow_ref, e_ref, s_ssd, s_hg,
                     bwd=True)
    yo_ref[:, :D_MODEL] = y.astype(BF16)
    yo_ref[:, D_MODEL:] = o.astype(BF16)


def _scan_scratch():
    return [pltpu.VMEM((SSD_STATE, D_MODEL), F32), pltpu.VMEM((HG_HEADDIM, D_MODEL), F32)]


def _scan_fwd(tables, n_seq, meta_step, act, xmeta, xhead, dt, qkv, lf3, cw, cb, arow, expand):
    t = act.shape[0]

    def rows(width, col=0):
        return pl.BlockSpec((TILE, width), lambda i, tile, *_: (tile[i], col))

    def const(shape):
        return pl.BlockSpec(shape, lambda i, *_: (0,) * len(shape))

    halo = (HALO_ROWS, SSD_CONV_DIM)
    in_specs = [
        rows(SSD_CONV_DIM),
        pl.BlockSpec(halo, lambda i, *_: (meta_step, 0)),
        pl.BlockSpec(halo, lambda i, tile, out, head, *_: (head[i], 0)),
        rows(LANES), rows(4 * D_MODEL), rows(3 * D_MODEL, 0),
        const((SUBLANES, SSD_CONV_DIM)), const((1, SSD_CONV_DIM)), const((1, LANES)), const((LANES, D_MODEL)),
    ]
    grid_spec = pltpu.PrefetchScalarGridSpec(
        num_scalar_prefetch=len(tables), grid=(len(tables[0]),), in_specs=in_specs,
        out_specs=pl.BlockSpec((TILE, 2 * D_MODEL), lambda i, tile, out, *_: (out[i], 0)),
        scratch_shapes=_scan_scratch() + [pltpu.VMEM((N_META, SSD_CONV_DIM), BF16)])
    return pl.pallas_call(
        _scan_fwd_body, grid_spec=grid_spec,
        out_shape=jax.ShapeDtypeStruct((t + n_seq * TILE, 2 * D_MODEL), BF16),
        compiler_params=_params("arbitrary"), name="scan_fwd",
    )(*tables, act, xmeta, xhead, dt, qkv, lf3, cw, cb, arow, expand)


def _scan_bwd(tables, act, dt, qkv, lf3, arow, expand):
    t = act.shape[0]

    def rows(width, col=0):
        return pl.BlockSpec((TILE, width), lambda i, tile, *_: (tile[i], col))

    def const(shape):
        return pl.BlockSpec(shape, lambda i, *_: (0,) * len(shape))

    in_specs = [rows(SSD_CONV_DIM), rows(LANES), rows(4 * D_MODEL), rows(3 * D_MODEL, 1),
                const((1, LANES)), const((LANES, D_MODEL))]
    grid_spec = pltpu.PrefetchScalarGridSpec(
        num_scalar_prefetch=len(tables), grid=(len(tables[0]),), in_specs=in_specs,
        out_specs=rows(2 * D_MODEL), scratch_shapes=_scan_scratch())
    return pl.pallas_call(
        _scan_bwd_body, grid_spec=grid_spec,
        out_shape=jax.ShapeDtypeStruct((t, 2 * D_MODEL), BF16),
        compiler_params=_params("arbitrary"), name="scan_bwd",
    )(*tables, act, dt, qkv, lf3, arow, expand)


def _merge_body(yof_ref, yob_ref, xs_ref, szsg_ref, gate_ref, h_ref, dexp_ref, snorm_ref, hnorm_ref,
                wa_ref, wb_ref, wo_ref, o_ref):
    dm = D_MODEL
    y = yof_ref[:, :dm].astype(F32) + yob_ref[:, :dm].astype(F32) + dexp_ref[...] * xs_ref[...].astype(F32)
    y = y * szsg_ref[:, :dm].astype(F32)
    bra = _rms(y, snorm_ref[...]).astype(BF16)
    o_parts = []
    for h in range(HG_HEADS):
        cols = slice(dm + h * HG_HEADDIM, dm + (h + 1) * HG_HEADDIM)
        oh = yof_ref[:, cols].astype(F32) + yob_ref[:, cols].astype(F32)
        o_parts.append(oh * lax.rsqrt(jnp.mean(oh * oh, axis=-1, keepdims=True) + EPS))
    brb = (jnp.concatenate(o_parts, axis=1) * hnorm_ref[...] * szsg_ref[:, dm:].astype(F32)).astype(BF16)
    a = _dot(bra, wa_ref[...])
    b = _dot(brb, wb_ref[...])
    merged = gate_ref[:, :dm].astype(F32) * a + gate_ref[:, dm:].astype(F32) * b
    o_ref[...] = h_ref[...] + _dot(merged.astype(BF16), wo_ref[...])


def _merge(rows, yo_f, yo_b, act, szsg, gates, h, dexp, snorm, hnorm, wa, wb, wo):
    def row(width):
        return pl.BlockSpec((MERGE_TILE, width), lambda i: (i, 0))

    sq = _resident((D_MODEL, D_MODEL))
    vec = _resident((1, D_MODEL))
    return pl.pallas_call(
        _merge_body,
        grid=(rows // MERGE_TILE,),
        in_specs=[row(2 * D_MODEL), row(2 * D_MODEL), row(D_MODEL), row(2 * D_MODEL), row(2 * D_MODEL),
                  row(D_MODEL), vec, vec, vec, sq, sq, sq],
        out_specs=row(D_MODEL),
        out_shape=jax.ShapeDtypeStruct((rows, D_MODEL), F32),
        compiler_params=_params("parallel"),
        name="merge",
    )(yo_f, yo_b, act, szsg, gates, h, dexp, snorm, hnorm, wa, wb, wo)


def _scan_tables(seq_rows, meta_row, n_tiles):
    meta_tile = meta_row // TILE
    f_tile, f_out, f_head, f_first, b_tile, b_reset = [], [], [], [], [], []
    row0 = 0
    for seq, rows in enumerate(seq_rows):
        t0, n = row0 // TILE, rows // TILE
        f_tile += [meta_tile] + [t0 + j for j in range(n)]
        f_out += [n_tiles + seq] + [t0 + j for j in range(n)]
        f_head += [row0 // MERGE_TILE] * (n + 1)
        f_first += [1] + [0] * n
        b_tile += [t0 + j for j in reversed(range(n))]
        b_reset += [1] + [0] * (n - 1)
        row0 += rows
    as_i32 = lambda xs: tuple(np.asarray(x, np.int32) for x in xs)
    return as_i32((f_tile, f_out, f_head, f_first)), as_i32((b_tile, b_reset))


def _inproj_tables(seq_rows, meta_row, n_steps):
    per_step = MERGE_TILE // SUBLANES
    prev, nxt, keep_prev, keep_next = [], [], [], []
    step = 0
    for rows in seq_rows:
        n = rows // MERGE_TILE
        for j in range(n):
            prev.append((meta_row + TILE) // SUBLANES - 1 if j == 0 else (step + j) * per_step - 1)
            nxt.append((step + j + 1) * per_step if j < n - 1 else 0)
            keep_prev.append(1)
            keep_next.append(1 if j < n - 1 else 0)
        step += n
    pad = n_steps - step
    as_i32 = lambda xs: tuple(np.asarray(x + [0] * pad, np.int32) for x in xs)
    return as_i32((prev, nxt, keep_prev, keep_next))


def kernel(x_prompt, x_sample, meta_tokens, ffn1_norm, ffn1_w_gate_up, ffn1_w_down, mix_norm, w_in, ssd_conv_w,
           ssd_conv_b, ssd_dt_bias, ssd_a_log, ssd_d, ssd_norm, ssd_w_proj, hg_lb_table, hg_norm, hg_w_proj, w_out,
           ffn2_norm, ffn2_w_gate_up, ffn2_w_down, final_norm):
    assert len(ffn1_norm) == 1, "single-layer block"
    groups = (x_prompt, x_sample)
    seq_rows = []
    for x in groups:
        b, s, dm = x.shape
        assert dm == D_MODEL and s % MERGE_TILE == 0 and (b * s) % ROW_TILE == 0
        seq_rows += [s] * b
    n_rows = sum(seq_rows)
    xp, xs = (x.reshape(-1, D_MODEL) for x in groups)
    tail = jnp.pad(meta_tokens.astype(F32), ((TILE - N_META, ROW_TILE - TILE), (0, 0)))
    fwd_tables, bwd_tables = _scan_tables(seq_rows, n_rows, (n_rows + ROW_TILE) // TILE)

    row = lambda v: v.reshape(1, -1).astype(F32)

    h1 = _ffn_in(xp, xs, tail, row(ffn1_norm[0]), ffn1_w_gate_up[0].astype(BF16), ffn1_w_down[0].astype(BF16))

    pts = np.cumsum((0,) + IN_WIDTHS)
    w_b = w_in[0].astype(BF16)
    z_w, xbc_w, dt_w, hq_w, hf_w, hi_w, hg_w, gate_w = (w_b[:, pts[j]:pts[j + 1]] for j in range(8))
    dt_w = jnp.pad(dt_w, ((0, 0), (0, LANES - 2 * SSD_HEADS)))
    w_packed = jnp.concatenate([z_w, xbc_w, hq_w, hf_w, hi_w, hg_w, gate_w, dt_w], axis=1)
    lb = jnp.cumsum(jax.nn.softmax(hg_lb_table.astype(F32), axis=1), axis=1)[:, 0].reshape(1, 2 * D_MODEL)
    dtb = jnp.pad(ssd_dt_bias[0].astype(F32).reshape(1, -1), ((0, 0), (0, LANES - 2 * SSD_HEADS)))
    cw = jnp.pad(ssd_conv_w[0].astype(F32), ((0, SUBLANES - SSD_CONV), (0, 0)))
    cb = row(ssd_conv_b[0])
    in_tables = _inproj_tables(seq_rows, n_rows, (n_rows + ROW_TILE) // MERGE_TILE)
    szsg, act, xhead, xmeta, qkv, lf3, gates, dt = _inproj(in_tables, h1, row(mix_norm[0]), w_packed, lb, dtb, cw, cb)

    arow = jnp.pad(-jnp.exp(ssd_a_log[0].astype(F32)).reshape(1, -1), ((0, 0), (0, LANES - 2 * SSD_HEADS)))
    head_of_col = np.arange(D_MODEL) // SSD_HEADDIM
    expands = [jnp.asarray((np.arange(LANES)[:, None] == head_of_col[None, :] + SSD_HEADS * d), BF16)
               for d in range(2)]
    dexp = jnp.repeat(ssd_d[0].astype(F32), SSD_HEADDIM).reshape(1, -1)

    yo_f = _scan_fwd(fwd_tables, len(seq_rows), n_rows // MERGE_TILE, act, xmeta, xhead, dt, qkv, lf3, cw, cb, arow,
                     expands[0])
    yo_b = _scan_bwd(bwd_tables, act, dt, qkv, lf3, arow, expands[1])

    h2 = _merge(n_rows, yo_f, yo_b, act, szsg, gates, h1, dexp, row(ssd_norm[0]), row(hg_norm[0]),
                ssd_w_proj[0].astype(BF16), hg_w_proj[0].astype(BF16), w_out[0].astype(BF16))

    wgu2, wd2 = ffn2_w_gate_up[0].astype(BF16), ffn2_w_down[0].astype(BF16)
    outs, row0 = [], 0
    for x in groups:
        b, s, dm = x.shape
        y = _ffn_out(h2, row0, b * s, row(ffn2_norm[0]), wgu2, wd2, row(final_norm))
        outs.append(y.reshape(b, s, dm))
        row0 += b * s
    return tuple(outs)
```

```python
import functools

import numpy as np
import jax
import jax.numpy as jnp
from jax import lax
from jax.experimental import pallas as pl
from jax.experimental.pallas import tpu as pltpu

F32 = jnp.float32
BF16 = jnp.bfloat16

D_MODEL = 1024
N_META = 16
EPS = 1e-6
D_FF = 2816
SSD_HEADS = 16
SSD_HEADDIM = 64
SSD_GROUPS = 4
SSD_STATE = 128
SSD_CONV = 5
SSD_GN = SSD_GROUPS * SSD_STATE
SSD_CONV_DIM = D_MODEL + 2 * SSD_GN
HG_HEADS = 8
HG_HEADDIM = 128
HG_CHUNK = 64
IN_WIDTHS = (D_MODEL, SSD_CONV_DIM, 2 * SSD_HEADS, D_MODEL, 2 * D_MODEL, D_MODEL, D_MODEL, 2 * D_MODEL)

TILE = 128
ROW_TILE = 512
MERGE_TILE = 256
FFN_CHUNKS = 11
PROJ_CHUNK = 256
HALO_ROWS = 16
SUBLANES = 8
LANES = 128
VMEM_LIMIT = 56 * 1024 * 1024


def _rms(x, w):
    return x * lax.rsqrt(jnp.mean(x * x, axis=-1, keepdims=True) + EPS) * w


def _dot(a, b):
    return jnp.dot(a, b, preferred_element_type=F32)


def _dot_nt(a, b):
    return lax.dot_general(a, b, (((1,), (1,)), ((), ())), preferred_element_type=F32)


def _dot_tn(a, b):
    return lax.dot_general(a, b, (((0,), (0,)), ((), ())), preferred_element_type=F32)


def _split3(x):
    x1 = x.astype(BF16)
    r1 = x - x1.astype(F32)
    x2 = r1.astype(BF16)
    x3 = (r1 - x2.astype(F32)).astype(BF16)
    return x1, x2, x3


def _dot01_rhs3(m01, x):
    x1, x2, x3 = _split3(x)
    return _dot(m01, x1) + _dot(m01, x2) + _dot(m01, x3)


def _dot01_lhs3(x, m01):
    x1, x2, x3 = _split3(x)
    return _dot(x1, m01) + _dot(x2, m01) + _dot(x3, m01)


def _resident(shape):
    return pl.BlockSpec(shape, lambda i, *_: (0,) * len(shape), pipeline_mode=pl.Buffered(1))


def _params(semantics):
    return pltpu.CompilerParams(dimension_semantics=(semantics,), vmem_limit_bytes=VMEM_LIMIT)


def _ffn_compute(x, nw_ref, wgu_ref, wd_ref, n_chunks):
    hn = _rms(x, nw_ref[...]).astype(BF16)
    tf = D_FF // n_chunks
    acc = jnp.zeros(x.shape, F32)
    for c in range(n_chunks):
        g = _dot(hn, wgu_ref[:, c * tf:(c + 1) * tf])
        u = _dot(hn, wgu_ref[:, D_FF + c * tf:D_FF + (c + 1) * tf])
        a = (jax.nn.silu(g) * u).astype(BF16)
        acc = acc + _dot(a, wd_ref[c * tf:(c + 1) * tf, :])
    return x + 0.5 * acc


def _ffn_in_body(xp_ref, xs_ref, tail_ref, nw_ref, wgu_ref, wd_ref, o_ref, *, steps_p, steps_s):
    i = pl.program_id(0)
    x = jnp.where(i < steps_p, xp_ref[...], jnp.where(i < steps_p + steps_s, xs_ref[...], tail_ref[...]))
    o_ref[...] = _ffn_compute(x, nw_ref, wgu_ref, wd_ref, FFN_CHUNKS)


def _ffn_in(xp, xs, tail, nw, wgu, wd):
    steps_p, steps_s = xp.shape[0] // ROW_TILE, xs.shape[0] // ROW_TILE
    steps = steps_p + steps_s + 1
    blk = (ROW_TILE, D_MODEL)
    return pl.pallas_call(
        functools.partial(_ffn_in_body, steps_p=steps_p, steps_s=steps_s),
        grid=(steps,),
        in_specs=[pl.BlockSpec(blk, lambda i: (jnp.minimum(i, steps_p - 1), 0)),
                  pl.BlockSpec(blk, lambda i: (jnp.clip(i - steps_p, 0, steps_s - 1), 0)),
                  pl.BlockSpec(blk, lambda i: (0, 0)),
                  _resident((1, D_MODEL)), _resident((D_MODEL, 2 * D_FF)), _resident((D_FF, D_MODEL))],
        out_specs=pl.BlockSpec(blk, lambda i: (i, 0)),
        out_shape=jax.ShapeDtypeStruct((steps * ROW_TILE, D_MODEL), F32),
        compiler_params=_params("parallel"),
        name="ffn_in",
    )(xp, xs, tail, nw, wgu, wd)


def _ffn_out_body(x_ref, nw_ref, wgu_ref, wd_ref, fnw_ref, o_ref):
    o_ref[...] = _rms(_ffn_compute(x_ref[...], nw_ref, wgu_ref, wd_ref, FFN_CHUNKS), fnw_ref[...])


def _ffn_out(h, row0, rows, nw, wgu, wd, fnw):
    blk = (ROW_TILE, D_MODEL)
    off = row0 // ROW_TILE
    return pl.pallas_call(
        _ffn_out_body,
        grid=(rows // ROW_TILE,),
        in_specs=[pl.BlockSpec(blk, lambda i: (i + off, 0)),
                  _resident((1, D_MODEL)), _resident((D_MODEL, 2 * D_FF)), _resident((D_FF, D_MODEL)),
                  _resident((1, D_MODEL))],
        out_specs=pl.BlockSpec(blk, lambda i: (i, 0)),
        out_shape=jax.ShapeDtypeStruct((rows, D_MODEL), F32),
        compiler_params=_params("parallel"),
        name="ffn_out",
    )(h, nw, wgu, wd, fnw)


_C_Z, _C_XBC, _C_Q, _C_F, _C_V, _C_G, _C_GATE, _C_DT, _C_END = (
    0, 1024, 3072, 4096, 6144, 7168, 8192, 10240, 10368)


def _inproj_body(prev_tbl, next_tbl, keep_prev_tbl, keep_next_tbl,
                 h_ref, hprev_ref, hnext_ref, nw_ref, w_ref, lb_ref, dtb_ref, cw_ref, cb_ref,
                 szsg_ref, act_ref, xhead_ref, xmeta_ref, qkv_ref, lf3_ref, gate_ref, dt_ref):
    i = pl.program_id(0)
    rows = h_ref.shape[0]
    h_ext = jnp.concatenate([h_ref[...], hprev_ref[...], hnext_ref[...]], axis=0)
    hn_ext = _rms(h_ext, nw_ref[...]).astype(BF16)
    hn = hn_ext[:rows]
    dm = D_MODEL

    cw = PROJ_CHUNK
    keep_prev = jnp.where(keep_prev_tbl[i] == 1, 1.0, 0.0)
    keep_next = jnp.where(keep_next_tbl[i] == 1, 1.0, 0.0)

    def proj(w0, c, lhs=hn):
        return _dot(lhs, w_ref[:, w0 + c:w0 + c + cw])

    def light_chunks():
        for c in range(0, dm, cw):
            szsg_ref[:, c:c + cw] = proj(_C_Z, c).astype(BF16)
            yield
        for c in range(0, dm, cw):
            qkv_ref[:, c:c + cw] = jax.nn.silu(proj(_C_Q, c)).astype(BF16)
            yield
        for d in range(2):
            for c in range(0, dm, cw):
                lb = lb_ref[:, d * dm + c:d * dm + c + cw]
                f = lb + (1.0 - lb) * jax.nn.sigmoid(proj(_C_F + d * dm, c))
                qkv_ref[:, (1 + d) * dm + c:(1 + d) * dm + c + cw] = (1.0 - f).astype(BF16)
                lf = jnp.log(f)
                hi = lf.astype(BF16)
                lf3_ref[:, 2 * d * dm + c:2 * d * dm + c + cw] = hi
                lf3_ref[:, (2 * d + 1) * dm + c:(2 * d + 1) * dm + c + cw] = (lf - hi.astype(F32)).astype(BF16)
                yield
        for c in range(0, dm, cw):
            qkv_ref[:, 3 * dm + c:3 * dm + c + cw] = proj(_C_V, c).astype(BF16)
            yield
        for c in range(0, dm, cw):
            szsg_ref[:, dm + c:dm + c + cw] = proj(_C_G, c).astype(BF16)
            yield
        for c in range(0, 2 * dm, cw):
            gate_ref[:, c:c + cw] = proj(_C_GATE, c).astype(BF16)
            yield

    light = light_chunks()
    n_conv = SSD_CONV_DIM // cw
    n_light = (_C_DT - SSD_CONV_DIM) // cw
    for k in range(n_conv):
        c = k * cw
        xe = proj(_C_XBC, c, hn_ext)
        for _ in range(n_light // n_conv):
            next(light)
        x = xe[:rows]
        act = _conv_silu(x, xe[rows:rows + SUBLANES] * keep_prev, xe[rows + SUBLANES:] * keep_next,
                         cw_ref[:, c:c + cw], cb_ref[:, c:c + cw])
        act_ref[:, c:c + cw] = act.astype(BF16)
        xhead_ref[:, c:c + cw] = x[:HALO_ROWS].astype(BF16)
        xmeta_ref[:, c:c + cw] = x[TILE - N_META:TILE].astype(BF16)
    for _ in light:
        pass
    dt_ref[...] = jax.nn.softplus(_dot(hn, w_ref[:, _C_DT:_C_END]) + dtb_ref[...])


def _inproj(tables, h, nw, w, lb, dtb, cw, cb):
    t = h.shape[0]
    steps = t // MERGE_TILE

    def row(width):
        return pl.BlockSpec((MERGE_TILE, width), lambda i, *_: (i, 0))

    def head(width):
        return pl.BlockSpec((HALO_ROWS, width), lambda i, *_: (i, 0))

    halo = (SUBLANES, D_MODEL)
    in_specs = [row(D_MODEL),
                pl.BlockSpec(halo, lambda i, prev, nxt, *_: (prev[i], 0)),
                pl.BlockSpec(halo, lambda i, prev, nxt, *_: (nxt[i], 0)),
                _resident((1, D_MODEL)), _resident((D_MODEL, _C_END)), _resident((1, 2 * D_MODEL)),
                _resident((1, LANES)), _resident((SUBLANES, SSD_CONV_DIM)), _resident((1, SSD_CONV_DIM))]
    out_specs = [row(2 * D_MODEL), row(SSD_CONV_DIM), head(SSD_CONV_DIM), head(SSD_CONV_DIM),
                 row(4 * D_MODEL), row(4 * D_MODEL), row(2 * D_MODEL), row(LANES)]
    out_shape = [jax.ShapeDtypeStruct((t, 2 * D_MODEL), BF16), jax.ShapeDtypeStruct((t, SSD_CONV_DIM), BF16),
                 jax.ShapeDtypeStruct((steps * HALO_ROWS, SSD_CONV_DIM), BF16),
                 jax.ShapeDtypeStruct((steps * HALO_ROWS, SSD_CONV_DIM), BF16),
                 jax.ShapeDtypeStruct((t, 4 * D_MODEL), BF16), jax.ShapeDtypeStruct((t, 4 * D_MODEL), BF16),
                 jax.ShapeDtypeStruct((t, 2 * D_MODEL), BF16), jax.ShapeDtypeStruct((t, LANES), F32)]
    grid_spec = pltpu.PrefetchScalarGridSpec(
        num_scalar_prefetch=len(tables), grid=(steps,), in_specs=in_specs, out_specs=out_specs)
    return pl.pallas_call(
        _inproj_body, grid_spec=grid_spec, out_shape=out_shape,
        compiler_params=_params("parallel"), name="inproj",
    )(*tables, h, h, h, nw, w, lb, dtb, cw, cb)


def _conv_silu(x, prev8, next8, cw, cb):
    half = SSD_CONV // 2
    sub = lax.broadcasted_iota(jnp.int32, (SUBLANES, 1), 0)
    acc = cb + cw[half:half + 1, :] * x
    for j in range(SSD_CONV):
        sh = half - j
        if sh == 0:
            continue
        rolled = pltpu.roll(x, sh % x.shape[0], axis=0)
        if sh > 0:
            fix = jnp.where(sub < sh, pltpu.roll(prev8, sh, axis=0), rolled[:SUBLANES])
            rolled = jnp.concatenate([fix, rolled[SUBLANES:]], axis=0)
        else:
            fix = jnp.where(sub >= SUBLANES + sh, pltpu.roll(next8, SUBLANES + sh, axis=0), rolled[-SUBLANES:])
            rolled = jnp.concatenate([rolled[:-SUBLANES], fix], axis=0)
        acc = acc + cw[j:j + 1, :] * rolled
    return jax.nn.silu(acc)


def _mix_tile(xs, bm, cm, dtv, qkv_ref, lf3_ref, arow_ref, e_ref, s_ssd, s_hg, *, bwd):
    hpg = SSD_HEADS // SSD_GROUPS
    gw = hpg * SSD_HEADDIM
    cq = HG_CHUNK
    n_chunks = TILE // cq
    chunk_order = list(reversed(range(n_chunks))) if bwd else list(range(n_chunks))
    g_cols = [slice(g * gw, (g + 1) * gw) for g in range(SSD_GROUPS)]
    g_state = [slice(g * SSD_STATE, (g + 1) * SSD_STATE) for g in range(SSD_GROUPS)]
    h_cols = [slice(h * HG_HEADDIM, (h + 1) * HG_HEADDIM) for h in range(HG_HEADS)]

    r_i = lax.broadcasted_iota(jnp.int32, (TILE, TILE), 0)
    c_i = lax.broadcasted_iota(jnp.int32, (TILE, TILE), 1)
    causal = (r_i <= c_i) if bwd else (r_i >= c_i)
    tri = jnp.where(causal, 1.0, 0.0).astype(BF16)
    causal64 = causal[:cq, :cq]
    tri_chunks = jnp.where(jnp.logical_and(causal, r_i // cq == c_i // cq), 1.0, 0.0).astype(BF16)
    dm = D_MODEL
    k0 = (2 if bwd else 1) * dm
    q_all = qkv_ref[:, :dm].astype(F32)
    k_all = qkv_ref[:, k0:k0 + dm].astype(F32)
    v_all = qkv_ref[:, 3 * dm:]
    end_row = 0 if bwd else TILE - 1
    ref_row = (cq - 1 - cq // 2) if bwd else cq // 2
    end64 = 0 if bwd else cq - 1

    s_old = s_ssd[...]
    s_old_b = s_old.astype(BF16)
    cbs = [_dot_nt(cm[:, g_state[g]], bm[:, g_state[g]]) for g in range(SSD_GROUPS)]
    y_offs = [_dot(cm[:, g_state[g]], s_old_b[:, g_cols[g]]) for g in range(SSD_GROUPS)]
    acs = _dot01_rhs3(tri, dtv * arow_ref[...])
    bcs_tile = _dot(tri_chunks, lf3_ref[:, :dm]) + _dot(tri_chunks, lf3_ref[:, dm:])
    bcs_all = [bcs_tile[ci * cq:(ci + 1) * cq] for ci in range(n_chunks)]

    total = acs[end_row:end_row + 1, :]
    e_in = jnp.exp(acs)
    e_end = jnp.exp(total - acs)
    expand = e_ref[...]
    dt_x = _dot(dtv.astype(BF16), expand)
    w_x = _dot((dtv * e_end).astype(BF16), expand)
    ein_x = _dot(e_in.astype(BF16), expand)
    etot_x = _dot01_lhs3(jnp.broadcast_to(jnp.exp(total), (SUBLANES, LANES)), expand)[0:1, :]
    acs_t = acs.T

    hg = []
    for ci in range(n_chunks):
        rows = slice(ci * cq, (ci + 1) * cq)
        qc, kc, bcs = q_all[rows], k_all[rows], bcs_all[ci]
        bref = bcs[ref_row:ref_row + 1, :]
        btot = bcs[end64:end64 + 1, :]
        qe = (qc * jnp.exp(bcs - bref)).astype(BF16)
        ke = (kc * jnp.exp(bref - bcs)).astype(BF16)
        kd = (kc * jnp.exp(btot - bcs)).astype(BF16)
        qd = (qc * jnp.exp(bcs)).astype(BF16)
        scores = [_dot_nt(qe[:, c], ke[:, c]) for c in h_cols]
        hg.append((scores, kd, qd, jnp.exp(btot), v_all[rows]))

    xdt = (xs * dt_x).astype(BF16)
    xdtd = (xs * w_x).astype(BF16)
    for g in range(SSD_GROUPS):
        s_ssd[:, g_cols[g]] = (s_old[:, g_cols[g]] * etot_x[:, g_cols[g]]
                               + _dot_tn(bm[:, g_state[g]], xdtd[:, g_cols[g]]))

    def hgrn_chunk(ci, st):
        scores, kd, qd, decay, vc = hg[ci]
        st_b = st.astype(BF16)
        inter = [_dot_nt(qd[:, c], st_b[:, c]) for c in h_cols]
        upd = [_dot_tn(vc[:, c], kd[:, c]) for c in h_cols]
        intra = [_dot(jnp.where(causal64, scores[h], 0.0).astype(BF16), vc[:, h_cols[h]]) for h in range(HG_HEADS)]
        o = jnp.concatenate([a + b for a, b in zip(intra, inter)], axis=1)
        return o, st * decay + jnp.concatenate(upd, axis=1)

    o_chunks = [None] * n_chunks
    o_chunks[chunk_order[0]], st = hgrn_chunk(chunk_order[0], s_hg[...])

    lane_i = lax.broadcasted_iota(jnp.int32, (TILE, LANES), 1)
    lo_half = lane_i < SSD_HEADDIM
    neg_inf = jnp.float32(-jnp.inf)
    y_parts = []
    for g in range(SSD_GROUPS):
        for pair in range(hpg // 2):
            ms = []
            for hh in range(2):
                lane = (SSD_HEADS if bwd else 0) + g * hpg + pair * 2 + hh
                seg = acs[:, lane:lane + 1] - acs_t[lane:lane + 1, :]
                ms.append((cbs[g] * jnp.exp(jnp.where(causal, seg, neg_inf))).astype(BF16))
            c0 = g * gw + pair * LANES
            xp = xdt[:, c0:c0 + LANES]
            zero = jnp.zeros_like(xp)
            rhs = jnp.concatenate([jnp.where(lo_half, xp, zero), jnp.where(lo_half, zero, xp)], axis=0)
            y_parts.append(_dot(jnp.concatenate(ms, axis=1), rhs))
    y = jnp.concatenate(y_parts, axis=1) + jnp.concatenate(y_offs, axis=1) * ein_x

    for ci in chunk_order[1:]:
        o_chunks[ci], st = hgrn_chunk(ci, st)
    s_hg[...] = st
    return y, jnp.concatenate(o_chunks, axis=0)


def _scan_fwd_body(tile_ref, out_ref, head_ref, first_ref,
                   act_ref, xmeta_ref, xhead_ref, dt_ref, qkv_ref, lf3_ref,
                   cw_ref, cb_ref, arow_ref, e_ref,
                   yo_ref, s_ssd, s_hg, meta_act):
    i = pl.program_id(0)
    first = first_ref[i] == 1

    @pl.when(first)
    def _():
        s_ssd[...] = jnp.zeros_like(s_ssd)
        s_hg[...] = jnp.zeros_like(s_hg)
        meta_act[...] = _conv_silu(xmeta_ref[...].astype(F32), jnp.zeros((SUBLANES, SSD_CONV_DIM), F32),
                                   xhead_ref[...].astype(F32)[:SUBLANES], cw_ref[...], cb_ref[...]).astype(BF16)

    n_pad = TILE - N_META
    act = jnp.concatenate([act_ref[:n_pad, :], jnp.where(first, meta_act[...], act_ref[n_pad:, :])], axis=0)
    row = lax.broadcasted_iota(jnp.int32, (TILE, 1), 0)
    valid = jnp.where(jnp.logical_and(first, row < n_pad), 0.0, 1.0)
    y, o = _mix_tile(act[:, :D_MODEL].astype(F32), act[:, D_MODEL:D_MODEL + SSD_GN], act[:, D_MODEL + SSD_GN:],
                     dt_ref[...] * valid, qkv_ref, lf3_ref, arow_ref, e_ref, s_ssd, s_hg, bwd=False)
    yo_ref[:, :D_MODEL] = y.astype(BF16)
    yo_ref[:, D_MODEL:] = o.astype(BF16)


def _scan_bwd_body(tile_ref, reset_ref, act_ref, dt_ref, qkv_ref, lf3_ref, arow_ref, e_ref, yo_ref, s_ssd, s_hg):
    i = pl.program_id(0)

    @pl.when(reset_ref[i] == 1)
    def _():
        s_ssd[...] = jnp.zeros_like(s_ssd)
        s_hg[...] = jnp.zeros_like(s_hg)

    y, o = _mix_tile(act_ref[:, :D_MODEL].astype(F32), act_ref[:, D_MODEL:D_MODEL + SSD_GN],
                     act_ref[:, D_MODEL + SSD_GN:], dt_ref[...], qkv_ref, lf3_ref, arow_ref, e_ref, s_ssd, s_hg,
                     bwd=True)
    yo_ref[:, :D_MODEL] = y.astype(BF16)
    yo_ref[:, D_MODEL:] = o.astype(BF16)


def _scan_scratch():
    return [pltpu.VMEM((SSD_STATE, D_MODEL), F32), pltpu.VMEM((HG_HEADDIM, D_MODEL), F32)]


def _scan_fwd(tables, n_seq, meta_step, act, xmeta, xhead, dt, qkv, lf3, cw, cb, arow, expand):
    t = act.shape[0]

    def rows(width, col=0):
        return pl.BlockSpec((TILE, width), lambda i, tile, *_: (tile[i], col))

    def const(shape):
        return pl.BlockSpec(shape, lambda i, *_: (0,) * len(shape))

    halo = (HALO_ROWS, SSD_CONV_DIM)
    in_specs = [
        rows(SSD_CONV_DIM),
        pl.BlockSpec(halo, lambda i, *_: (meta_step, 0)),
        pl.BlockSpec(halo, lambda i, tile, out, head, *_: (head[i], 0)),
        rows(LANES), rows(4 * D_MODEL), rows(2 * D_MODEL, 0),
        const((SUBLANES, SSD_CONV_DIM)), const((1, SSD_CONV_DIM)), const((1, LANES)), const((LANES, D_MODEL)),
    ]
    grid_spec = pltpu.PrefetchScalarGridSpec(
        num_scalar_prefetch=len(tables), grid=(len(tables[0]),), in_specs=in_specs,
        out_specs=pl.BlockSpec((TILE, 2 * D_MODEL), lambda i, tile, out, *_: (out[i], 0)),
        scratch_shapes=_scan_scratch() + [pltpu.VMEM((N_META, SSD_CONV_DIM), BF16)])
    return pl.pallas_call(
        _scan_fwd_body, grid_spec=grid_spec,
        out_shape=jax.ShapeDtypeStruct((t + n_seq * TILE, 2 * D_MODEL), BF16),
        compiler_params=_params("arbitrary"), name="scan_fwd",
    )(*tables, act, xmeta, xhead, dt, qkv, lf3, cw, cb, arow, expand)


def _scan_bwd(tables, act, dt, qkv, lf3, arow, expand):
    t = act.shape[0]

    def rows(width, col=0):
        return pl.BlockSpec((TILE, width), lambda i, tile, *_: (tile[i], col))

    def const(shape):
        return pl.BlockSpec(shape, lambda i, *_: (0,) * len(shape))

    in_specs = [rows(SSD_CONV_DIM), rows(LANES), rows(4 * D_MODEL), rows(2 * D_MODEL, 1),
                const((1, LANES)), const((LANES, D_MODEL))]
    grid_spec = pltpu.PrefetchScalarGridSpec(
        num_scalar_prefetch=len(tables), grid=(len(tables[0]),), in_specs=in_specs,
        out_specs=rows(2 * D_MODEL), scratch_shapes=_scan_scratch())
    return pl.pallas_call(
        _scan_bwd_body, grid_spec=grid_spec,
        out_shape=jax.ShapeDtypeStruct((t, 2 * D_MODEL), BF16),
        compiler_params=_params("arbitrary"), name="scan_bwd",
    )(*tables, act, dt, qkv, lf3, arow, expand)


def _merge_body(yof_ref, yob_ref, xs_ref, szsg_ref, gate_ref, h_ref, dexp_ref, snorm_ref, hnorm_ref,
                wa_ref, wb_ref, wo_ref, o_ref):
    dm = D_MODEL
    y = yof_ref[:, :dm].astype(F32) + yob_ref[:, :dm].astype(F32) + dexp_ref[...] * xs_ref[...].astype(F32)
    y = y * jax.nn.silu(szsg_ref[:, :dm].astype(F32))
    bra = _rms(y, snorm_ref[...]).astype(BF16)
    o_parts = []
    for h in range(HG_HEADS):
        cols = slice(dm + h * HG_HEADDIM, dm + (h + 1) * HG_HEADDIM)
        oh = yof_ref[:, cols].astype(F32) + yob_ref[:, cols].astype(F32)
        o_parts.append(oh * lax.rsqrt(jnp.mean(oh * oh, axis=-1, keepdims=True) + EPS))
    sg = jax.nn.silu(szsg_ref[:, dm:].astype(F32))
    brb = (jnp.concatenate(o_parts, axis=1) * hnorm_ref[...] * sg).astype(BF16)
    a = _dot(bra, wa_ref[...])
    b = _dot(brb, wb_ref[...])
    merged = (jax.nn.sigmoid(gate_ref[:, :dm].astype(F32)) * a
              + jax.nn.sigmoid(gate_ref[:, dm:].astype(F32)) * b)
    o_ref[...] = h_ref[...] + _dot(merged.astype(BF16), wo_ref[...])


def _merge(rows, yo_f, yo_b, act, szsg, gates, h, dexp, snorm, hnorm, wa, wb, wo):
    def row(width):
        return pl.BlockSpec((ROW_TILE, width), lambda i: (i, 0))

    sq = _resident((D_MODEL, D_MODEL))
    vec = _resident((1, D_MODEL))
    return pl.pallas_call(
        _merge_body,
        grid=(rows // ROW_TILE,),
        in_specs=[row(2 * D_MODEL), row(2 * D_MODEL), row(D_MODEL), row(2 * D_MODEL), row(2 * D_MODEL),
                  row(D_MODEL), vec, vec, vec, sq, sq, sq],
        out_specs=row(D_MODEL),
        out_shape=jax.ShapeDtypeStruct((rows, D_MODEL), F32),
        compiler_params=_params("parallel"),
        name="merge",
    )(yo_f, yo_b, act, szsg, gates, h, dexp, snorm, hnorm, wa, wb, wo)


def _scan_tables(seq_rows, meta_row, n_tiles):
    meta_tile = meta_row // TILE
    f_tile, f_out, f_head, f_first, b_tile, b_reset = [], [], [], [], [], []
    row0 = 0
    for seq, rows in enumerate(seq_rows):
        t0, n = row0 // TILE, rows // TILE
        f_tile += [meta_tile] + [t0 + j for j in range(n)]
        f_out += [n_tiles + seq] + [t0 + j for j in range(n)]
        f_head += [row0 // MERGE_TILE] * (n + 1)
        f_first += [1] + [0] * n
        b_tile += [t0 + j for j in reversed(range(n))]
        b_reset += [1] + [0] * (n - 1)
        row0 += rows
    as_i32 = lambda xs: tuple(np.asarray(x, np.int32) for x in xs)
    return as_i32((f_tile, f_out, f_head, f_first)), as_i32((b_tile, b_reset))


def _inproj_tables(seq_rows, meta_row, n_steps):
    per_step = MERGE_TILE // SUBLANES
    prev, nxt, keep_prev, keep_next = [], [], [], []
    step = 0
    for rows in seq_rows:
        n = rows // MERGE_TILE
        for j in range(n):
            prev.append((meta_row + TILE) // SUBLANES - 1 if j == 0 else (step + j) * per_step - 1)
            nxt.append((step + j + 1) * per_step if j < n - 1 else 0)
            keep_prev.append(1)
            keep_next.append(1 if j < n - 1 else 0)
        step += n
    pad = n_steps - step
    as_i32 = lambda xs: tuple(np.asarray(x + [0] * pad, np.int32) for x in xs)
    return as_i32((prev, nxt, keep_prev, keep_next))


def kernel(x_prompt, x_sample, meta_tokens, ffn1_norm, ffn1_w_gate_up, ffn1_w_down, mix_norm, w_in, ssd_conv_w,
           ssd_conv_b, ssd_dt_bias, ssd_a_log, ssd_d, ssd_norm, ssd_w_proj, hg_lb_table, hg_norm, hg_w_proj, w_out,
           ffn2_norm, ffn2_w_gate_up, ffn2_w_down, final_norm):
    assert len(ffn1_norm) == 1, "single-layer block"
    groups = (x_prompt, x_sample)
    seq_rows = []
    for x in groups:
        b, s, dm = x.shape
        assert dm == D_MODEL and s % MERGE_TILE == 0 and (b * s) % ROW_TILE == 0
        seq_rows += [s] * b
    n_rows = sum(seq_rows)
    xp, xs = (x.reshape(-1, D_MODEL) for x in groups)
    tail = jnp.pad(meta_tokens.astype(F32), ((TILE - N_META, ROW_TILE - TILE), (0, 0)))
    fwd_tables, bwd_tables = _scan_tables(seq_rows, n_rows, (n_rows + ROW_TILE) // TILE)

    row = lambda v: v.reshape(1, -1).astype(F32)

    h1 = _ffn_in(xp, xs, tail, row(ffn1_norm[0]), ffn1_w_gate_up[0].astype(BF16), ffn1_w_down[0].astype(BF16))

    pts = np.cumsum((0,) + IN_WIDTHS)
    w_b = w_in[0].astype(BF16)
    dt_w = jnp.pad(w_b[:, pts[2]:pts[3]], ((0, 0), (0, LANES - 2 * SSD_HEADS)))
    w_packed = jnp.concatenate([w_b[:, :pts[2]], w_b[:, pts[3]:], dt_w], axis=1)
    lb = jnp.cumsum(jax.nn.softmax(hg_lb_table.astype(F32), axis=1), axis=1)[:, 0].reshape(1, 2 * D_MODEL)
    dtb = jnp.pad(ssd_dt_bias[0].astype(F32).reshape(1, -1), ((0, 0), (0, LANES - 2 * SSD_HEADS)))
    cw = jnp.pad(ssd_conv_w[0].astype(F32), ((0, SUBLANES - SSD_CONV), (0, 0)))
    cb = row(ssd_conv_b[0])
    in_tables = _inproj_tables(seq_rows, n_rows, (n_rows + ROW_TILE) // MERGE_TILE)
    szsg, act, xhead, xmeta, qkv, lf3, gates, dt = _inproj(in_tables, h1, row(mix_norm[0]), w_packed, lb, dtb, cw, cb)

    arow = jnp.pad(-jnp.exp(ssd_a_log[0].astype(F32)).reshape(1, -1), ((0, 0), (0, LANES - 2 * SSD_HEADS)))
    head_of_col = np.arange(D_MODEL) // SSD_HEADDIM
    expands = [jnp.asarray((np.arange(LANES)[:, None] == head_of_col[None, :] + SSD_HEADS * d), BF16)
               for d in range(2)]
    dexp = jnp.repeat(ssd_d[0].astype(F32), SSD_HEADDIM).reshape(1, -1)

    yo_f = _scan_fwd(fwd_tables, len(seq_rows), n_rows // MERGE_TILE, act, xmeta, xhead, dt, qkv, lf3, cw, cb, arow,
                     expands[0])
    yo_b = _scan_bwd(bwd_tables, act, dt, qkv, lf3, arow, expands[1])

    h2 = _merge(n_rows, yo_f, yo_b, act, szsg, gates, h1, dexp, row(ssd_norm[0]), row(hg_norm[0]),
                ssd_w_proj[0].astype(BF16), hg_w_proj[0].astype(BF16), w_out[0].astype(BF16))

    wgu2, wd2 = ffn2_w_gate_up[0].astype(BF16), ffn2_w_down[0].astype(BF16)
    outs, row0 = [], 0
    for x in groups:
        b, s, dm = x.shape
        y = _ffn_out(h2, row0, b * s, row(ffn2_norm[0]), wgu2, wd2, row(final_norm))
        outs.append(y.reshape(b, s, dm))
        row0 += b * s
    return tuple(outs)
```

```python
import functools

import numpy as np
import jax
import jax.numpy as jnp
from jax import lax
from jax.experimental import pallas as pl
from jax.experimental.pallas import tpu as pltpu

F32 = jnp.float32
BF16 = jnp.bfloat16

D_MODEL = 1024
N_META = 16
EPS = 1e-6
D_FF = 2816
SSD_HEADS = 16
SSD_HEADDIM = 64
SSD_GROUPS = 4
SSD_STATE = 128
SSD_CONV = 5
SSD_GN = SSD_GROUPS * SSD_STATE
SSD_CONV_DIM = D_MODEL + 2 * SSD_GN
HG_HEADS = 8
HG_HEADDIM = 128
HG_CHUNK = 64
IN_WIDTHS = (D_MODEL, SSD_CONV_DIM, 2 * SSD_HEADS, D_MODEL, 2 * D_MODEL, D_MODEL, D_MODEL, 2 * D_MODEL)

TILE = 128
ROW_TILE = 512
MERGE_TILE = 256
FFN_CHUNKS = 11
PROJ_CHUNK = 256
HALO_ROWS = 16
SUBLANES = 8
LANES = 128
VMEM_LIMIT = 56 * 1024 * 1024
LOG2E = 1.4426950408889634


def _rms(x, w):
    return x * lax.rsqrt(jnp.mean(x * x, axis=-1, keepdims=True) + EPS) * w


def _dot(a, b):
    return jnp.dot(a, b, preferred_element_type=F32)


def _dot_nt(a, b):
    return lax.dot_general(a, b, (((1,), (1,)), ((), ())), preferred_element_type=F32)


def _dot_tn(a, b):
    return lax.dot_general(a, b, (((0,), (0,)), ((), ())), preferred_element_type=F32)


def _split3(x):
    x1 = x.astype(BF16)
    r1 = x - x1.astype(F32)
    x2 = r1.astype(BF16)
    x3 = (r1 - x2.astype(F32)).astype(BF16)
    return x1, x2, x3


def _dot01_rhs3(m01, x):
    x1, x2, x3 = _split3(x)
    return _dot(m01, x1) + _dot(m01, x2) + _dot(m01, x3)


def _dot01_lhs3(x, m01):
    x1, x2, x3 = _split3(x)
    return _dot(x1, m01) + _dot(x2, m01) + _dot(x3, m01)


def _resident(shape):
    return pl.BlockSpec(shape, lambda i, *_: (0,) * len(shape), pipeline_mode=pl.Buffered(1))


def _params(semantics):
    return pltpu.CompilerParams(dimension_semantics=(semantics,), vmem_limit_bytes=VMEM_LIMIT)


def _ffn_compute(x, nw_ref, wgu_ref, wd_ref, n_chunks):
    hn = _rms(x, nw_ref[...]).astype(BF16)
    tf = D_FF // n_chunks
    acc = jnp.zeros(x.shape, F32)
    for c in range(n_chunks):
        g = _dot(hn, wgu_ref[:, c * tf:(c + 1) * tf])
        u = _dot(hn, wgu_ref[:, D_FF + c * tf:D_FF + (c + 1) * tf])
        a = (jax.nn.silu(g) * u).astype(BF16)
        acc = acc + _dot(a, wd_ref[c * tf:(c + 1) * tf, :])
    return x + 0.5 * acc


def _ffn_in_body(xp_ref, xs_ref, tail_ref, nw_ref, wgu_ref, wd_ref, o_ref, *, steps_p, steps_s):
    i = pl.program_id(0)
    x = jnp.where(i < steps_p, xp_ref[...], jnp.where(i < steps_p + steps_s, xs_ref[...], tail_ref[...]))
    o_ref[...] = _ffn_compute(x, nw_ref, wgu_ref, wd_ref, FFN_CHUNKS)


def _ffn_in(xp, xs, tail, nw, wgu, wd):
    steps_p, steps_s = xp.shape[0] // ROW_TILE, xs.shape[0] // ROW_TILE
    steps = steps_p + steps_s + 1
    blk = (ROW_TILE, D_MODEL)
    return pl.pallas_call(
        functools.partial(_ffn_in_body, steps_p=steps_p, steps_s=steps_s),
        grid=(steps,),
        in_specs=[pl.BlockSpec(blk, lambda i: (jnp.minimum(i, steps_p - 1), 0)),
                  pl.BlockSpec(blk, lambda i: (jnp.clip(i - steps_p, 0, steps_s - 1), 0)),
                  pl.BlockSpec(blk, lambda i: (0, 0)),
                  _resident((1, D_MODEL)), _resident((D_MODEL, 2 * D_FF)), _resident((D_FF, D_MODEL))],
        out_specs=pl.BlockSpec(blk, lambda i: (i, 0)),
        out_shape=jax.ShapeDtypeStruct((steps * ROW_TILE, D_MODEL), F32),
        compiler_params=_params("parallel"),
        name="ffn_in",
    )(xp, xs, tail, nw, wgu, wd)


def _ffn_out_body(x_ref, nw_ref, wgu_ref, wd_ref, fnw_ref, o_ref):
    o_ref[...] = _rms(_ffn_compute(x_ref[...], nw_ref, wgu_ref, wd_ref, FFN_CHUNKS), fnw_ref[...])


def _ffn_out(h, row0, rows, nw, wgu, wd, fnw):
    blk = (ROW_TILE, D_MODEL)
    off = row0 // ROW_TILE
    return pl.pallas_call(
        _ffn_out_body,
        grid=(rows // ROW_TILE,),
        in_specs=[pl.BlockSpec(blk, lambda i: (i + off, 0)),
                  _resident((1, D_MODEL)), _resident((D_MODEL, 2 * D_FF)), _resident((D_FF, D_MODEL)),
                  _resident((1, D_MODEL))],
        out_specs=pl.BlockSpec(blk, lambda i: (i, 0)),
        out_shape=jax.ShapeDtypeStruct((rows, D_MODEL), F32),
        compiler_params=_params("parallel"),
        name="ffn_out",
    )(h, nw, wgu, wd, fnw)


_C_Z, _C_XBC, _C_Q, _C_F, _C_V, _C_G, _C_GATE, _C_DT, _C_END = (
    0, 1024, 3072, 4096, 6144, 7168, 8192, 10240, 10368)


def _inproj_body(prev_tbl, next_tbl, keep_prev_tbl, keep_next_tbl,
                 h_ref, hprev_ref, hnext_ref, nw_ref, w_ref, lb_ref, dtb_ref, cw_ref, cb_ref,
                 szsg_ref, act_ref, xhead_ref, xmeta_ref, qkv_ref, lf3_ref, gate_ref, dt_ref):
    i = pl.program_id(0)
    rows = h_ref.shape[0]
    h_ext = jnp.concatenate([h_ref[...], hprev_ref[...], hnext_ref[...]], axis=0)
    hn_ext = _rms(h_ext, nw_ref[...]).astype(BF16)
    hn = hn_ext[:rows]
    dm = D_MODEL

    cw = PROJ_CHUNK
    keep_prev = jnp.where(keep_prev_tbl[i] == 1, 1.0, 0.0)
    keep_next = jnp.where(keep_next_tbl[i] == 1, 1.0, 0.0)

    def proj(w0, c, lhs=hn):
        return _dot(lhs, w_ref[:, w0 + c:w0 + c + cw])

    def light_chunks():
        for c in range(0, dm, cw):
            szsg_ref[:, c:c + cw] = proj(_C_Z, c).astype(BF16)
            yield
        for c in range(0, dm, cw):
            qkv_ref[:, c:c + cw] = jax.nn.silu(proj(_C_Q, c)).astype(BF16)
            yield
        for d in range(2):
            for c in range(0, dm, cw):
                lb = lb_ref[:, d * dm + c:d * dm + c + cw]
                f = lb + (1.0 - lb) * jax.nn.sigmoid(proj(_C_F + d * dm, c))
                qkv_ref[:, (1 + d) * dm + c:(1 + d) * dm + c + cw] = (1.0 - f).astype(BF16)
                lf = jnp.log(f)
                hi = lf.astype(BF16)
                lf3_ref[:, 2 * d * dm + c:2 * d * dm + c + cw] = hi
                lf3_ref[:, (2 * d + 1) * dm + c:(2 * d + 1) * dm + c + cw] = (lf - hi.astype(F32)).astype(BF16)
                yield
        for c in range(0, dm, cw):
            qkv_ref[:, 3 * dm + c:3 * dm + c + cw] = proj(_C_V, c).astype(BF16)
            yield
        for c in range(0, dm, cw):
            szsg_ref[:, dm + c:dm + c + cw] = proj(_C_G, c).astype(BF16)
            yield
        for c in range(0, 2 * dm, cw):
            gate_ref[:, c:c + cw] = proj(_C_GATE, c).astype(BF16)
            yield

    light = light_chunks()
    n_conv = SSD_CONV_DIM // cw
    n_light = (_C_DT - SSD_CONV_DIM) // cw
    for k in range(n_conv):
        c = k * cw
        xe = proj(_C_XBC, c, hn_ext)
        for _ in range(n_light // n_conv):
            next(light)
        x = xe[:rows]
        act = _conv_silu(x, xe[rows:rows + SUBLANES] * keep_prev, xe[rows + SUBLANES:] * keep_next,
                         cw_ref[:, c:c + cw], cb_ref[:, c:c + cw])
        act_ref[:, c:c + cw] = act.astype(BF16)
        xhead_ref[:, c:c + cw] = x[:HALO_ROWS].astype(BF16)
        xmeta_ref[:, c:c + cw] = x[TILE - N_META:TILE].astype(BF16)
    for _ in light:
        pass
    dt_ref[...] = jax.nn.softplus(_dot(hn, w_ref[:, _C_DT:_C_END]) + dtb_ref[...])


def _inproj(tables, h, nw, w, lb, dtb, cw, cb):
    t = h.shape[0]
    steps = t // MERGE_TILE

    def row(width):
        return pl.BlockSpec((MERGE_TILE, width), lambda i, *_: (i, 0))

    def head(width):
        return pl.BlockSpec((HALO_ROWS, width), lambda i, *_: (i, 0))

    halo = (SUBLANES, D_MODEL)
    in_specs = [row(D_MODEL),
                pl.BlockSpec(halo, lambda i, prev, nxt, *_: (prev[i], 0)),
                pl.BlockSpec(halo, lambda i, prev, nxt, *_: (nxt[i], 0)),
                _resident((1, D_MODEL)), _resident((D_MODEL, _C_END)), _resident((1, 2 * D_MODEL)),
                _resident((1, LANES)), _resident((SUBLANES, SSD_CONV_DIM)), _resident((1, SSD_CONV_DIM))]
    out_specs = [row(2 * D_MODEL), row(SSD_CONV_DIM), head(SSD_CONV_DIM), head(SSD_CONV_DIM),
                 row(4 * D_MODEL), row(4 * D_MODEL), row(2 * D_MODEL), row(LANES)]
    out_shape = [jax.ShapeDtypeStruct((t, 2 * D_MODEL), BF16), jax.ShapeDtypeStruct((t, SSD_CONV_DIM), BF16),
                 jax.ShapeDtypeStruct((steps * HALO_ROWS, SSD_CONV_DIM), BF16),
                 jax.ShapeDtypeStruct((steps * HALO_ROWS, SSD_CONV_DIM), BF16),
                 jax.ShapeDtypeStruct((t, 4 * D_MODEL), BF16), jax.ShapeDtypeStruct((t, 4 * D_MODEL), BF16),
                 jax.ShapeDtypeStruct((t, 2 * D_MODEL), BF16), jax.ShapeDtypeStruct((t, LANES), F32)]
    grid_spec = pltpu.PrefetchScalarGridSpec(
        num_scalar_prefetch=len(tables), grid=(steps,), in_specs=in_specs, out_specs=out_specs)
    return pl.pallas_call(
        _inproj_body, grid_spec=grid_spec, out_shape=out_shape,
        compiler_params=_params("parallel"), name="inproj",
    )(*tables, h, h, h, nw, w, lb, dtb, cw, cb)


def _conv_silu(x, prev8, next8, cw, cb):
    half = SSD_CONV // 2
    sub = lax.broadcasted_iota(jnp.int32, (SUBLANES, 1), 0)
    acc = cb + cw[half:half + 1, :] * x
    for j in range(SSD_CONV):
        sh = half - j
        if sh == 0:
            continue
        rolled = pltpu.roll(x, sh % x.shape[0], axis=0)
        if sh > 0:
            fix = jnp.where(sub < sh, pltpu.roll(prev8, sh, axis=0), rolled[:SUBLANES])
            rolled = jnp.concatenate([fix, rolled[SUBLANES:]], axis=0)
        else:
            fix = jnp.where(sub >= SUBLANES + sh, pltpu.roll(next8, SUBLANES + sh, axis=0), rolled[-SUBLANES:])
            rolled = jnp.concatenate([rolled[:-SUBLANES], fix], axis=0)
        acc = acc + cw[j:j + 1, :] * rolled
    return jax.nn.silu(acc)


def _mix_tile(xs, bm, cm, dtv, qkv_ref, lf3_ref, arow_ref, e_ref, s_ssd, s_hg, *, bwd):
    hpg = SSD_HEADS // SSD_GROUPS
    gw = hpg * SSD_HEADDIM
    cq = HG_CHUNK
    n_chunks = TILE // cq
    chunk_order = list(reversed(range(n_chunks))) if bwd else list(range(n_chunks))
    g_cols = [slice(g * gw, (g + 1) * gw) for g in range(SSD_GROUPS)]
    g_state = [slice(g * SSD_STATE, (g + 1) * SSD_STATE) for g in range(SSD_GROUPS)]
    h_cols = [slice(h * HG_HEADDIM, (h + 1) * HG_HEADDIM) for h in range(HG_HEADS)]

    r_i = lax.broadcasted_iota(jnp.int32, (TILE, TILE), 0)
    c_i = lax.broadcasted_iota(jnp.int32, (TILE, TILE), 1)
    causal = (r_i <= c_i) if bwd else (r_i >= c_i)
    tri = jnp.where(causal, 1.0, 0.0).astype(BF16)
    causal64 = causal[:cq, :cq]
    tri_chunks = jnp.where(jnp.logical_and(causal, r_i // cq == c_i // cq), 1.0, 0.0).astype(BF16)
    dm = D_MODEL
    k0 = (2 if bwd else 1) * dm
    q_all = qkv_ref[:, :dm]
    k_all = qkv_ref[:, k0:k0 + dm]
    v_all = qkv_ref[:, 3 * dm:]
    end_row = 0 if bwd else TILE - 1
    ref_row = (cq - 1 - cq // 2) if bwd else cq // 2
    end64 = 0 if bwd else cq - 1

    s_old = s_ssd[...]
    s_old_b = s_old.astype(BF16)
    cbs = [_dot_nt(cm[:, g_state[g]], bm[:, g_state[g]]) for g in range(SSD_GROUPS)]
    y_offs = [_dot(cm[:, g_state[g]], s_old_b[:, g_cols[g]]) for g in range(SSD_GROUPS)]
    acs = _dot01_rhs3(tri, dtv * arow_ref[...]) * LOG2E
    bcs_tile = (_dot(tri_chunks, lf3_ref[:, :dm]) + _dot(tri_chunks, lf3_ref[:, dm:])) * LOG2E
    bcs_all = [bcs_tile[ci * cq:(ci + 1) * cq] for ci in range(n_chunks)]
    yield

    total = acs[end_row:end_row + 1, :]
    e_in = jnp.exp2(acs)
    e_end = jnp.exp2(total - acs)
    expand = e_ref[...]
    dt_x = _dot(dtv.astype(BF16), expand)
    w_x = _dot((dtv * e_end).astype(BF16), expand)
    ein_x = _dot(e_in.astype(BF16), expand)
    etot_x = _dot01_lhs3(jnp.broadcast_to(jnp.exp2(total), (SUBLANES, LANES)), expand)[0:1, :]
    acs_t = acs.T
    yield

    hg = []
    for ci in range(n_chunks):
        rows = slice(ci * cq, (ci + 1) * cq)
        qc, kc, bcs = q_all[rows], k_all[rows], bcs_all[ci]
        bref = bcs[ref_row:ref_row + 1, :]
        btot = bcs[end64:end64 + 1, :]
        qe = qc * jnp.exp2(bcs - bref).astype(BF16)
        ke = kc * jnp.exp2(bref - bcs).astype(BF16)
        kd = kc * jnp.exp2(btot - bcs).astype(BF16)
        qd = qc * jnp.exp2(bcs).astype(BF16)
        scores = [_dot_nt(qe[:, c], ke[:, c]) for c in h_cols]
        hg.append((scores, kd, qd, jnp.exp2(btot), v_all[rows]))
        yield

    xdt = (xs * dt_x).astype(BF16)
    xdtd = (xs * w_x).astype(BF16)
    for g in range(SSD_GROUPS):
        s_ssd[:, g_cols[g]] = (s_old[:, g_cols[g]] * etot_x[:, g_cols[g]]
                               + _dot_tn(bm[:, g_state[g]], xdtd[:, g_cols[g]]))
    yield

    def hgrn_chunk(ci, st):
        scores, kd, qd, decay, vc = hg[ci]
        st_b = st.astype(BF16)
        inter = [_dot_nt(qd[:, c], st_b[:, c]) for c in h_cols]
        upd = [_dot_tn(vc[:, c], kd[:, c]) for c in h_cols]
        intra = [_dot(jnp.where(causal64, scores[h], 0.0).astype(BF16), vc[:, h_cols[h]]) for h in range(HG_HEADS)]
        o = jnp.concatenate([a + b for a, b in zip(intra, inter)], axis=1)
        return o, st * decay + jnp.concatenate(upd, axis=1)

    o_chunks = [None] * n_chunks
    o_chunks[chunk_order[0]], st = hgrn_chunk(chunk_order[0], s_hg[...])
    yield

    lane_i = lax.broadcasted_iota(jnp.int32, (TILE, LANES), 1)
    lo_half = lane_i < SSD_HEADDIM
    neg_inf = jnp.float32(-jnp.inf)
    y_parts = []
    for g in range(SSD_GROUPS):
        for pair in range(hpg // 2):
            ms = []
            for hh in range(2):
                lane = (SSD_HEADS if bwd else 0) + g * hpg + pair * 2 + hh
                seg = acs[:, lane:lane + 1] - acs_t[lane:lane + 1, :]
                ms.append((cbs[g] * jnp.exp2(jnp.where(causal, seg, neg_inf))).astype(BF16))
            c0 = g * gw + pair * LANES
            xp = xdt[:, c0:c0 + LANES]
            zero = jnp.zeros_like(xp)
            rhs = jnp.concatenate([jnp.where(lo_half, xp, zero), jnp.where(lo_half, zero, xp)], axis=0)
            y_parts.append(_dot(jnp.concatenate(ms, axis=1), rhs))
    y = jnp.concatenate(y_parts, axis=1) + jnp.concatenate(y_offs, axis=1) * ein_x
    yield

    for ci in chunk_order[1:]:
        o_chunks[ci], st = hgrn_chunk(ci, st)
    s_hg[...] = st
    return y, jnp.concatenate(o_chunks, axis=0)


def _scan_body(f_tile, f_out, f_head, f_first, b_tile, b_out, b_reset,
               actf_ref, xmeta_ref, xhead_ref, dtf_ref, qkvf_ref, lff_ref,
               actb_ref, dtb_ref, qkvb_ref, lfb_ref,
               cw_ref, cb_ref, arow_ref, ef_ref, eb_ref,
               yof_ref, yob_ref, sf_ssd, sf_hg, sb_ssd, sb_hg, meta_act):
    i = pl.program_id(0)
    first = f_first[i] == 1

    @pl.when(first)
    def _():
        sf_ssd[...] = jnp.zeros_like(sf_ssd)
        sf_hg[...] = jnp.zeros_like(sf_hg)
        meta_act[...] = _conv_silu(xmeta_ref[...].astype(F32), jnp.zeros((SUBLANES, SSD_CONV_DIM), F32),
                                   xhead_ref[...].astype(F32)[:SUBLANES], cw_ref[...], cb_ref[...]).astype(BF16)

    @pl.when(b_reset[i] == 1)
    def _():
        sb_ssd[...] = jnp.zeros_like(sb_ssd)
        sb_hg[...] = jnp.zeros_like(sb_hg)

    n_pad = TILE - N_META
    act = jnp.concatenate([actf_ref[:n_pad, :], jnp.where(first, meta_act[...], actf_ref[n_pad:, :])], axis=0)
    row = lax.broadcasted_iota(jnp.int32, (TILE, 1), 0)
    valid = jnp.where(jnp.logical_and(first, row < n_pad), 0.0, 1.0)
    dm, gn = D_MODEL, SSD_GN
    directions = [
        _mix_tile(act[:, :dm].astype(F32), act[:, dm:dm + gn], act[:, dm + gn:], dtf_ref[...] * valid,
                  qkvf_ref, lff_ref, arow_ref, ef_ref, sf_ssd, sf_hg, bwd=False),
        _mix_tile(actb_ref[:, :dm].astype(F32), actb_ref[:, dm:dm + gn], actb_ref[:, dm + gn:], dtb_ref[...],
                  qkvb_ref, lfb_ref, arow_ref, eb_ref, sb_ssd, sb_hg, bwd=True),
    ]
    results = [None] * len(directions)
    while any(r is None for r in results):
        for d, stages in enumerate(directions):
            if results[d] is None:
                try:
                    next(stages)
                except StopIteration as done:
                    results[d] = done.value
    for (y, o), yo_ref in zip(results, (yof_ref, yob_ref)):
        yo_ref[:, :dm] = y.astype(BF16)
        yo_ref[:, dm:] = o.astype(BF16)


def _scan(tables, n_spare, meta_step, act, xmeta, xhead, dt, qkv, lf3, cw, cb, arow, expands):
    t = act.shape[0]

    def fwd(width, col=0):
        return pl.BlockSpec((TILE, width), lambda i, f_tile, *_: (f_tile[i], col))

    def bwd(width, col=0):
        return pl.BlockSpec((TILE, width), lambda i, f_tile, f_out, f_head, f_first, b_tile, *_: (b_tile[i], col))

    def const(shape):
        return pl.BlockSpec(shape, lambda i, *_: (0,) * len(shape))

    halo = (HALO_ROWS, SSD_CONV_DIM)
    in_specs = [
        fwd(SSD_CONV_DIM),
        pl.BlockSpec(halo, lambda i, *_: (meta_step, 0)),
        pl.BlockSpec(halo, lambda i, f_tile, f_out, f_head, *_: (f_head[i], 0)),
        fwd(LANES), fwd(4 * D_MODEL), fwd(2 * D_MODEL, 0),
        bwd(SSD_CONV_DIM), bwd(LANES), bwd(4 * D_MODEL), bwd(2 * D_MODEL, 1),
        const((SUBLANES, SSD_CONV_DIM)), const((1, SSD_CONV_DIM)), const((1, LANES)),
        const((LANES, D_MODEL)), const((LANES, D_MODEL)),
    ]
    out_specs = [
        pl.BlockSpec((TILE, 2 * D_MODEL), lambda i, f_tile, f_out, *_: (f_out[i], 0)),
        pl.BlockSpec((TILE, 2 * D_MODEL),
                     lambda i, f_tile, f_out, f_head, f_first, b_tile, b_out, *_: (b_out[i], 0)),
    ]
    state = [pltpu.VMEM((SSD_STATE, D_MODEL), F32), pltpu.VMEM((HG_HEADDIM, D_MODEL), F32)]
    grid_spec = pltpu.PrefetchScalarGridSpec(
        num_scalar_prefetch=len(tables), grid=(len(tables[0]),), in_specs=in_specs, out_specs=out_specs,
        scratch_shapes=state + state + [pltpu.VMEM((N_META, SSD_CONV_DIM), BF16)])
    yo = jax.ShapeDtypeStruct((t + n_spare * TILE, 2 * D_MODEL), BF16)
    return pl.pallas_call(
        _scan_body, grid_spec=grid_spec, out_shape=[yo, yo],
        compiler_params=_params("arbitrary"), name="scan",
    )(*tables, act, xmeta, xhead, dt, qkv, lf3, act, dt, qkv, lf3, cw, cb, arow, *expands)


def _merge_body(yof_ref, yob_ref, xs_ref, szsg_ref, gate_ref, h_ref, dexp_ref, snorm_ref, hnorm_ref,
                wa_ref, wb_ref, wo_ref, o_ref):
    dm = D_MODEL
    y = yof_ref[:, :dm].astype(F32) + yob_ref[:, :dm].astype(F32) + dexp_ref[...] * xs_ref[...].astype(F32)
    y = y * jax.nn.silu(szsg_ref[:, :dm].astype(F32))
    bra = _rms(y, snorm_ref[...]).astype(BF16)
    o_parts = []
    for h in range(HG_HEADS):
        cols = slice(dm + h * HG_HEADDIM, dm + (h + 1) * HG_HEADDIM)
        oh = yof_ref[:, cols].astype(F32) + yob_ref[:, cols].astype(F32)
        o_parts.append(oh * lax.rsqrt(jnp.mean(oh * oh, axis=-1, keepdims=True) + EPS))
    sg = jax.nn.silu(szsg_ref[:, dm:].astype(F32))
    brb = (jnp.concatenate(o_parts, axis=1) * hnorm_ref[...] * sg).astype(BF16)
    a = _dot(bra, wa_ref[...])
    b = _dot(brb, wb_ref[...])
    merged = (jax.nn.sigmoid(gate_ref[:, :dm].astype(F32)) * a
              + jax.nn.sigmoid(gate_ref[:, dm:].astype(F32)) * b)
    o_ref[...] = h_ref[...] + _dot(merged.astype(BF16), wo_ref[...])


def _merge(rows, yo_f, yo_b, act, szsg, gates, h, dexp, snorm, hnorm, wa, wb, wo):
    def row(width):
        return pl.BlockSpec((ROW_TILE, width), lambda i: (i, 0))

    sq = _resident((D_MODEL, D_MODEL))
    vec = _resident((1, D_MODEL))
    return pl.pallas_call(
        _merge_body,
        grid=(rows // ROW_TILE,),
        in_specs=[row(2 * D_MODEL), row(2 * D_MODEL), row(D_MODEL), row(2 * D_MODEL), row(2 * D_MODEL),
                  row(D_MODEL), vec, vec, vec, sq, sq, sq],
        out_specs=row(D_MODEL),
        out_shape=jax.ShapeDtypeStruct((rows, D_MODEL), F32),
        compiler_params=_params("parallel"),
        name="merge",
    )(yo_f, yo_b, act, szsg, gates, h, dexp, snorm, hnorm, wa, wb, wo)


def _scan_tables(seq_rows, meta_row, n_tiles):
    meta_tile = meta_row // TILE
    f_tile, f_out, f_head, f_first, b_tile, b_reset = [], [], [], [], [], []
    row0 = 0
    for seq, rows in enumerate(seq_rows):
        t0, n = row0 // TILE, rows // TILE
        f_tile += [meta_tile] + [t0 + j for j in range(n)]
        f_out += [n_tiles + seq] + [t0 + j for j in range(n)]
        f_head += [row0 // MERGE_TILE] * (n + 1)
        f_first += [1] + [0] * n
        b_tile += [t0 + j for j in reversed(range(n))]
        b_reset += [1] + [0] * (n - 1)
        row0 += rows
    n_seq, n_idle = len(seq_rows), len(f_tile) - len(b_tile)
    b_out = b_tile + [n_tiles + n_seq + j for j in range(n_idle)]
    b_tile = b_tile + [b_tile[-1]] * n_idle
    b_reset = b_reset + [0] * n_idle
    as_i32 = lambda xs: tuple(np.asarray(x, np.int32) for x in xs)
    return as_i32((f_tile, f_out, f_head, f_first, b_tile, b_out, b_reset)), n_seq + n_idle


def _inproj_tables(seq_rows, meta_row, n_steps):
    per_step = MERGE_TILE // SUBLANES
    prev, nxt, keep_prev, keep_next = [], [], [], []
    step = 0
    for rows in seq_rows:
        n = rows // MERGE_TILE
        for j in range(n):
            prev.append((meta_row + TILE) // SUBLANES - 1 if j == 0 else (step + j) * per_step - 1)
            nxt.append((step + j + 1) * per_step if j < n - 1 else 0)
            keep_prev.append(1)
            keep_next.append(1 if j < n - 1 else 0)
        step += n
    pad = n_steps - step
    as_i32 = lambda xs: tuple(np.asarray(x + [0] * pad, np.int32) for x in xs)
    return as_i32((prev, nxt, keep_prev, keep_next))


def kernel(x_prompt, x_sample, meta_tokens, ffn1_norm, ffn1_w_gate_up, ffn1_w_down, mix_norm, w_in, ssd_conv_w,
           ssd_conv_b, ssd_dt_bias, ssd_a_log, ssd_d, ssd_norm, ssd_w_proj, hg_lb_table, hg_norm, hg_w_proj, w_out,
           ffn2_norm, ffn2_w_gate_up, ffn2_w_down, final_norm):
    assert len(ffn1_norm) == 1, "single-layer block"
    groups = (x_prompt, x_sample)
    seq_rows = []
    for x in groups:
        b, s, dm = x.shape
        assert dm == D_MODEL and s % MERGE_TILE == 0 and (b * s) % ROW_TILE == 0
        seq_rows += [s] * b
    n_rows = sum(seq_rows)
    xp, xs = (x.reshape(-1, D_MODEL) for x in groups)
    tail = jnp.pad(meta_tokens.astype(F32), ((TILE - N_META, ROW_TILE - TILE), (0, 0)))
    scan_tables, n_spare = _scan_tables(seq_rows, n_rows, (n_rows + ROW_TILE) // TILE)

    row = lambda v: v.reshape(1, -1).astype(F32)

    h1 = _ffn_in(xp, xs, tail, row(ffn1_norm[0]), ffn1_w_gate_up[0].astype(BF16), ffn1_w_down[0].astype(BF16))

    pts = np.cumsum((0,) + IN_WIDTHS)
    w_f = w_in[0]
    dt_w = jnp.pad(w_f[:, pts[2]:pts[3]], ((0, 0), (0, LANES - 2 * SSD_HEADS)))
    w_packed = jnp.concatenate([w_f[:, :pts[2]], w_f[:, pts[3]:], dt_w], axis=1).astype(BF16)
    lb = jnp.cumsum(jax.nn.softmax(hg_lb_table.astype(F32), axis=1), axis=1)[:, 0].reshape(1, 2 * D_MODEL)
    dtb = jnp.pad(ssd_dt_bias[0].astype(F32).reshape(1, -1), ((0, 0), (0, LANES - 2 * SSD_HEADS)))
    cw = jnp.pad(ssd_conv_w[0].astype(F32), ((0, SUBLANES - SSD_CONV), (0, 0)))
    cb = row(ssd_conv_b[0])
    in_tables = _inproj_tables(seq_rows, n_rows, (n_rows + ROW_TILE) // MERGE_TILE)
    szsg, act, xhead, xmeta, qkv, lf3, gates, dt = _inproj(in_tables, h1, row(mix_norm[0]), w_packed, lb, dtb, cw, cb)

    arow = jnp.pad(-jnp.exp(ssd_a_log[0].astype(F32)).reshape(1, -1), ((0, 0), (0, LANES - 2 * SSD_HEADS)))
    head_of_col = np.arange(D_MODEL) // SSD_HEADDIM
    expands = [jnp.asarray((np.arange(LANES)[:, None] == head_of_col[None, :] + SSD_HEADS * d), BF16)
               for d in range(2)]
    dexp = jnp.repeat(ssd_d[0].astype(F32), SSD_HEADDIM).reshape(1, -1)

    yo_f, yo_b = _scan(scan_tables, n_spare, n_rows // MERGE_TILE, act, xmeta, xhead, dt, qkv, lf3, cw, cb, arow,
                       expands)

    h2 = _merge(n_rows, yo_f, yo_b, act, szsg, gates, h1, dexp, row(ssd_norm[0]), row(hg_norm[0]),
                ssd_w_proj[0].astype(BF16), hg_w_proj[0].astype(BF16), w_out[0].astype(BF16))

    wgu2, wd2 = ffn2_w_gate_up[0].astype(BF16), ffn2_w_down[0].astype(BF16)
    outs, row0 = [], 0
    for x in groups:
        b, s, dm = x.shape
        y = _ffn_out(h2, row0, b * s, row(ffn2_norm[0]), wgu2, wd2, row(final_norm))
        outs.append(y.reshape(b, s, dm))
        row0 += b * s
    return tuple(outs)
```

```python
import functools

import numpy as np
import jax
import jax.numpy as jnp
from jax import lax
from jax.experimental import pallas as pl
from jax.experimental.pallas import tpu as pltpu

F32 = jnp.float32
BF16 = jnp.bfloat16

D_MODEL = 1024
N_META = 16
EPS = 1e-6
D_FF = 2816
SSD_HEADS = 16
SSD_HEADDIM = 64
SSD_GROUPS = 4
SSD_STATE = 128
SSD_CONV = 5
SSD_GN = SSD_GROUPS * SSD_STATE
SSD_CONV_DIM = D_MODEL + 2 * SSD_GN
HG_HEADS = 8
HG_HEADDIM = 128
HG_CHUNK = 64
IN_WIDTHS = (D_MODEL, SSD_CONV_DIM, 2 * SSD_HEADS, D_MODEL, 2 * D_MODEL, D_MODEL, D_MODEL, 2 * D_MODEL)

TILE = 128
ROW_TILE = 512
MERGE_TILE = 256
FFN_CHUNKS = 11
PROJ_CHUNK = 256
HALO_ROWS = 16
SUBLANES = 8
LANES = 128
VMEM_LIMIT = 56 * 1024 * 1024
LOG2E = 1.4426950408889634
SCAN_LEAD = 5


def _rms(x, w):
    return x * lax.rsqrt(jnp.mean(x * x, axis=-1, keepdims=True) + EPS) * w


def _dot(a, b):
    return jnp.dot(a, b, preferred_element_type=F32)


def _dot_nt(a, b):
    return lax.dot_general(a, b, (((1,), (1,)), ((), ())), preferred_element_type=F32)


def _dot_tn(a, b):
    return lax.dot_general(a, b, (((0,), (0,)), ((), ())), preferred_element_type=F32)


def _split3(x):
    x1 = x.astype(BF16)
    r1 = x - x1.astype(F32)
    x2 = r1.astype(BF16)
    x3 = (r1 - x2.astype(F32)).astype(BF16)
    return x1, x2, x3


def _dot01_rhs3(m01, x):
    x1, x2, x3 = _split3(x)
    return _dot(m01, x1) + _dot(m01, x2) + _dot(m01, x3)


def _dot01_lhs3(x, m01):
    x1, x2, x3 = _split3(x)
    return _dot(x1, m01) + _dot(x2, m01) + _dot(x3, m01)


def _resident(shape):
    return pl.BlockSpec(shape, lambda i, *_: (0,) * len(shape), pipeline_mode=pl.Buffered(1))


def _params(semantics):
    return pltpu.CompilerParams(dimension_semantics=(semantics,), vmem_limit_bytes=VMEM_LIMIT)


def _ffn_compute(x, nw_ref, wgu_ref, wd_ref, n_chunks):
    hn = _rms(x, nw_ref[...]).astype(BF16)
    tf = D_FF // n_chunks
    acc = jnp.zeros(x.shape, F32)
    for c in range(n_chunks):
        g = _dot(hn, wgu_ref[:, c * tf:(c + 1) * tf])
        u = _dot(hn, wgu_ref[:, D_FF + c * tf:D_FF + (c + 1) * tf])
        a = (jax.nn.silu(g) * u).astype(BF16)
        acc = acc + _dot(a, wd_ref[c * tf:(c + 1) * tf, :])
    return x + 0.5 * acc


def _ffn_in_body(xp_ref, xs_ref, tail_ref, nw_ref, wgu_ref, wd_ref, o_ref, *, steps_p, steps_s):
    i = pl.program_id(0)
    x = jnp.where(i < steps_p, xp_ref[...], jnp.where(i < steps_p + steps_s, xs_ref[...], tail_ref[...]))
    o_ref[...] = _ffn_compute(x, nw_ref, wgu_ref, wd_ref, FFN_CHUNKS)


def _ffn_in(xp, xs, tail, nw, wgu, wd):
    steps_p, steps_s = xp.shape[0] // ROW_TILE, xs.shape[0] // ROW_TILE
    steps = steps_p + steps_s + 1
    blk = (ROW_TILE, D_MODEL)
    return pl.pallas_call(
        functools.partial(_ffn_in_body, steps_p=steps_p, steps_s=steps_s),
        grid=(steps,),
        in_specs=[pl.BlockSpec(blk, lambda i: (jnp.minimum(i, steps_p - 1), 0)),
                  pl.BlockSpec(blk, lambda i: (jnp.clip(i - steps_p, 0, steps_s - 1), 0)),
                  pl.BlockSpec(blk, lambda i: (0, 0)),
                  _resident((1, D_MODEL)), _resident((D_MODEL, 2 * D_FF)), _resident((D_FF, D_MODEL))],
        out_specs=pl.BlockSpec(blk, lambda i: (i, 0)),
        out_shape=jax.ShapeDtypeStruct((steps * ROW_TILE, D_MODEL), F32),
        compiler_params=_params("parallel"),
        name="ffn_in",
    )(xp, xs, tail, nw, wgu, wd)


def _ffn_out_body(x_ref, nw_ref, wgu_ref, wd_ref, fnw_ref, o_ref):
    o_ref[...] = _rms(_ffn_compute(x_ref[...], nw_ref, wgu_ref, wd_ref, FFN_CHUNKS), fnw_ref[...])


def _ffn_out(h, row0, rows, nw, wgu, wd, fnw):
    blk = (ROW_TILE, D_MODEL)
    off = row0 // ROW_TILE
    return pl.pallas_call(
        _ffn_out_body,
        grid=(rows // ROW_TILE,),
        in_specs=[pl.BlockSpec(blk, lambda i: (i + off, 0)),
                  _resident((1, D_MODEL)), _resident((D_MODEL, 2 * D_FF)), _resident((D_FF, D_MODEL)),
                  _resident((1, D_MODEL))],
        out_specs=pl.BlockSpec(blk, lambda i: (i, 0)),
        out_shape=jax.ShapeDtypeStruct((rows, D_MODEL), F32),
        compiler_params=_params("parallel"),
        name="ffn_out",
    )(h, nw, wgu, wd, fnw)


_C_Z, _C_XBC, _C_Q, _C_F, _C_V, _C_G, _C_GATE, _C_DT, _C_END = (
    0, 1024, 3072, 4096, 6144, 7168, 8192, 10240, 10368)


def _inproj_body(prev_tbl, next_tbl, keep_prev_tbl, keep_next_tbl,
                 h_ref, hprev_ref, hnext_ref, nw_ref, w_ref, lb_ref, dtb_ref, cw_ref, cb_ref,
                 szsg_ref, act_ref, xhead_ref, xmeta_ref, qkv_ref, vt_ref, lf3_ref, gate_ref, dt_ref):
    i = pl.program_id(0)
    rows = h_ref.shape[0]
    h_ext = jnp.concatenate([h_ref[...], hprev_ref[...], hnext_ref[...]], axis=0)
    hn_ext = _rms(h_ext, nw_ref[...]).astype(BF16)
    hn = hn_ext[:rows]
    dm = D_MODEL

    cw = PROJ_CHUNK
    keep_prev = jnp.where(keep_prev_tbl[i] == 1, 1.0, 0.0)
    keep_next = jnp.where(keep_next_tbl[i] == 1, 1.0, 0.0)

    def proj(w0, c, lhs=hn):
        return _dot(lhs, w_ref[:, w0 + c:w0 + c + cw])

    def light_chunks():
        for c in range(0, dm, cw):
            szsg_ref[:, c:c + cw] = proj(_C_Z, c).astype(BF16)
            yield
        for c in range(0, dm, cw):
            qkv_ref[:, c:c + cw] = jax.nn.silu(proj(_C_Q, c)).astype(BF16)
            yield
        for d in range(2):
            for c in range(0, dm, cw):
                lb = lb_ref[:, d * dm + c:d * dm + c + cw]
                f = lb + (1.0 - lb) * jax.nn.sigmoid(proj(_C_F + d * dm, c))
                qkv_ref[:, (1 + d) * dm + c:(1 + d) * dm + c + cw] = (1.0 - f).astype(BF16)
                lf = jnp.log(f)
                hi = lf.astype(BF16)
                lf3_ref[:, 2 * d * dm + c:2 * d * dm + c + cw] = hi
                lf3_ref[:, (2 * d + 1) * dm + c:(2 * d + 1) * dm + c + cw] = (lf - hi.astype(F32)).astype(BF16)
                yield
        for c in range(0, dm, cw):
            v = proj(_C_V, c)
            qkv_ref[:, 3 * dm + c:3 * dm + c + cw] = v.astype(BF16)
            vt = v.T.astype(BF16)
            for j in range(rows // TILE):
                vt_ref[j, c:c + cw, :] = vt[:, j * TILE:(j + 1) * TILE]
            yield
        for c in range(0, dm, cw):
            szsg_ref[:, dm + c:dm + c + cw] = proj(_C_G, c).astype(BF16)
            yield
        for c in range(0, 2 * dm, cw):
            gate_ref[:, c:c + cw] = proj(_C_GATE, c).astype(BF16)
            yield

    light = light_chunks()
    n_conv = SSD_CONV_DIM // cw
    n_light = (_C_DT - SSD_CONV_DIM) // cw
    for k in range(n_conv):
        c = k * cw
        xe = proj(_C_XBC, c, hn_ext)
        for _ in range(n_light // n_conv):
            next(light)
        x = xe[:rows]
        act = _conv_silu(x, xe[rows:rows + SUBLANES] * keep_prev, xe[rows + SUBLANES:] * keep_next,
                         cw_ref[:, c:c + cw], cb_ref[:, c:c + cw])
        act_ref[:, c:c + cw] = act.astype(BF16)
        xhead_ref[:, c:c + cw] = x[:HALO_ROWS].astype(BF16)
        xmeta_ref[:, c:c + cw] = x[TILE - N_META:TILE].astype(BF16)
    for _ in light:
        pass
    dt_ref[...] = jax.nn.softplus(_dot(hn, w_ref[:, _C_DT:_C_END]) + dtb_ref[...])


def _inproj(tables, h, nw, w, lb, dtb, cw, cb):
    t = h.shape[0]
    steps = t // MERGE_TILE

    def row(width):
        return pl.BlockSpec((MERGE_TILE, width), lambda i, *_: (i, 0))

    def head(width):
        return pl.BlockSpec((HALO_ROWS, width), lambda i, *_: (i, 0))

    halo = (SUBLANES, D_MODEL)
    in_specs = [row(D_MODEL),
                pl.BlockSpec(halo, lambda i, prev, nxt, *_: (prev[i], 0)),
                pl.BlockSpec(halo, lambda i, prev, nxt, *_: (nxt[i], 0)),
                _resident((1, D_MODEL)), _resident((D_MODEL, _C_END)), _resident((1, 2 * D_MODEL)),
                _resident((1, LANES)), _resident((SUBLANES, SSD_CONV_DIM)), _resident((1, SSD_CONV_DIM))]
    out_specs = [row(2 * D_MODEL), row(SSD_CONV_DIM), head(SSD_CONV_DIM), head(SSD_CONV_DIM),
                 row(4 * D_MODEL), pl.BlockSpec((MERGE_TILE // TILE, D_MODEL, TILE), lambda i, *_: (i, 0, 0)),
                 row(4 * D_MODEL), row(2 * D_MODEL), row(LANES)]
    out_shape = [jax.ShapeDtypeStruct((t, 2 * D_MODEL), BF16), jax.ShapeDtypeStruct((t, SSD_CONV_DIM), BF16),
                 jax.ShapeDtypeStruct((steps * HALO_ROWS, SSD_CONV_DIM), BF16),
                 jax.ShapeDtypeStruct((steps * HALO_ROWS, SSD_CONV_DIM), BF16),
                 jax.ShapeDtypeStruct((t, 4 * D_MODEL), BF16), jax.ShapeDtypeStruct((t // TILE, D_MODEL, TILE), BF16),
                 jax.ShapeDtypeStruct((t, 4 * D_MODEL), BF16),
                 jax.ShapeDtypeStruct((t, 2 * D_MODEL), BF16), jax.ShapeDtypeStruct((t, LANES), F32)]
    grid_spec = pltpu.PrefetchScalarGridSpec(
        num_scalar_prefetch=len(tables), grid=(steps,), in_specs=in_specs, out_specs=out_specs)
    return pl.pallas_call(
        _inproj_body, grid_spec=grid_spec, out_shape=out_shape,
        compiler_params=_params("parallel"), name="inproj",
    )(*tables, h, h, h, nw, w, lb, dtb, cw, cb)


def _conv_silu(x, prev8, next8, cw, cb):
    half = SSD_CONV // 2
    sub = lax.broadcasted_iota(jnp.int32, (SUBLANES, 1), 0)
    acc = cb + cw[half:half + 1, :] * x
    for j in range(SSD_CONV):
        sh = half - j
        if sh == 0:
            continue
        rolled = pltpu.roll(x, sh % x.shape[0], axis=0)
        if sh > 0:
            fix = jnp.where(sub < sh, pltpu.roll(prev8, sh, axis=0), rolled[:SUBLANES])
            rolled = jnp.concatenate([fix, rolled[SUBLANES:]], axis=0)
        else:
            fix = jnp.where(sub >= SUBLANES + sh, pltpu.roll(next8, SUBLANES + sh, axis=0), rolled[-SUBLANES:])
            rolled = jnp.concatenate([rolled[:-SUBLANES], fix], axis=0)
        acc = acc + cw[j:j + 1, :] * rolled
    return jax.nn.silu(acc)


def _mix_tile(xs, bm, cm, dtv, qkv_ref, vt_ref, lf3_ref, arow_ref, e_ref, s_ssd, s_hg, *, bwd):
    hpg = SSD_HEADS // SSD_GROUPS
    gw = hpg * SSD_HEADDIM
    cq = HG_CHUNK
    n_chunks = TILE // cq
    chunk_order = list(reversed(range(n_chunks))) if bwd else list(range(n_chunks))
    g_cols = [slice(g * gw, (g + 1) * gw) for g in range(SSD_GROUPS)]
    g_state = [slice(g * SSD_STATE, (g + 1) * SSD_STATE) for g in range(SSD_GROUPS)]
    h_cols = [slice(h * HG_HEADDIM, (h + 1) * HG_HEADDIM) for h in range(HG_HEADS)]

    r_i = lax.broadcasted_iota(jnp.int32, (TILE, TILE), 0)
    c_i = lax.broadcasted_iota(jnp.int32, (TILE, TILE), 1)
    causal = (r_i <= c_i) if bwd else (r_i >= c_i)
    tri = jnp.where(causal, 1.0, 0.0).astype(BF16)
    causal64 = causal[:cq, :cq]
    tri_chunks = jnp.where(jnp.logical_and(causal, r_i // cq == c_i // cq), 1.0, 0.0).astype(BF16)
    dm = D_MODEL
    k0 = (2 if bwd else 1) * dm
    q_all = qkv_ref[:, :dm]
    k_all = qkv_ref[:, k0:k0 + dm]
    v_all = qkv_ref[:, 3 * dm:]
    end_row = 0 if bwd else TILE - 1
    ref_row = (cq - 1 - cq // 2) if bwd else cq // 2
    end64 = 0 if bwd else cq - 1

    s_old = s_ssd[...]
    s_old_b = s_old.astype(BF16)
    cbs = [_dot_nt(cm[:, g_state[g]], bm[:, g_state[g]]) for g in range(SSD_GROUPS)]
    y_offs = [_dot(cm[:, g_state[g]], s_old_b[:, g_cols[g]]) for g in range(SSD_GROUPS)]
    acs = _dot01_rhs3(tri, dtv * arow_ref[...]) * LOG2E
    bcs_tile = (_dot(tri_chunks, lf3_ref[:, :dm]) + _dot(tri_chunks, lf3_ref[:, dm:])) * LOG2E
    bcs_all = [bcs_tile[ci * cq:(ci + 1) * cq] for ci in range(n_chunks)]
    yield

    total = acs[end_row:end_row + 1, :]
    e_in = jnp.exp2(acs)
    e_end = jnp.exp2(total - acs)
    expand = e_ref[...]
    dt_x = _dot(dtv.astype(BF16), expand)
    w_x = _dot((dtv * e_end).astype(BF16), expand)
    ein_x = _dot(e_in.astype(BF16), expand)
    etot_x = _dot01_lhs3(jnp.broadcast_to(jnp.exp2(total), (SUBLANES, LANES)), expand)[0:1, :]
    acs_t = acs.T
    yield

    ca, cb_ = chunk_order
    hg = {}
    for ci in chunk_order:
        rows = slice(ci * cq, (ci + 1) * cq)
        qc, kc, bcs = q_all[rows], k_all[rows], bcs_all[ci]
        bref = bcs[ref_row:ref_row + 1, :]
        btot = bcs[end64:end64 + 1, :]
        qe = qc * jnp.exp2(bcs - bref).astype(BF16)
        ke = kc * jnp.exp2(bref - bcs).astype(BF16)
        kd = kc * jnp.exp2(btot - bcs).astype(BF16)
        qd = qc * jnp.exp2(bcs).astype(BF16)
        scores = [_dot_nt(qe[:, c], ke[:, c]) for c in h_cols]
        hg[ci] = (scores, kd, qd, jnp.exp2(btot), v_all[rows])
        yield
    scores_a, kd_a, qd_a, decay_a, v_a = hg[ca]
    scores_b, kd_b, qd_b, decay_b, v_b = hg[cb_]
    cross = [_dot_nt(qd_b[:, c], kd_a[:, c]) for c in h_cols]
    in_order = (lambda a, b: [a, b]) if ca < cb_ else (lambda a, b: [b, a])
    qd_tile = jnp.concatenate(in_order(qd_a, qd_b * decay_a.astype(BF16)), axis=0)
    kd_tile = jnp.concatenate(in_order(kd_a * decay_b.astype(BF16), kd_b), axis=0)
    yield

    xdt = (xs * dt_x).astype(BF16)
    xdtd = (xs * w_x).astype(BF16)
    for g in range(SSD_GROUPS):
        s_ssd[:, g_cols[g]] = (s_old[:, g_cols[g]] * etot_x[:, g_cols[g]]
                               + _dot_tn(bm[:, g_state[g]], xdtd[:, g_cols[g]]))
    yield

    st = s_hg[...]
    st_b = st.astype(BF16)
    inter = [_dot_nt(qd_tile[:, c], st_b[:, c]) for c in h_cols]
    upd = [_dot(vt_ref[0, c, :], kd_tile[:, c]) for c in h_cols]
    s_hg[...] = st * (decay_a * decay_b) + jnp.concatenate(upd, axis=1)
    yield
    o_parts = []
    for h, c in enumerate(h_cols):
        intra_a = _dot(jnp.where(causal64, scores_a[h], 0.0).astype(BF16), v_a[:, c])
        intra_b = (_dot(jnp.where(causal64, scores_b[h], 0.0).astype(BF16), v_b[:, c])
                   + _dot(cross[h].astype(BF16), v_a[:, c]))
        o_parts.append(jnp.concatenate(in_order(intra_a, intra_b), axis=0) + inter[h])
    o = jnp.concatenate(o_parts, axis=1)
    yield

    lane_i = lax.broadcasted_iota(jnp.int32, (TILE, LANES), 1)
    lo_half = lane_i < SSD_HEADDIM
    neg_inf = jnp.float32(-jnp.inf)
    y_parts = []
    for g in range(SSD_GROUPS):
        for pair in range(hpg // 2):
            ms = []
            for hh in range(2):
                lane = (SSD_HEADS if bwd else 0) + g * hpg + pair * 2 + hh
                seg = acs[:, lane:lane + 1] - acs_t[lane:lane + 1, :]
                ms.append((cbs[g] * jnp.exp2(jnp.where(causal, seg, neg_inf))).astype(BF16))
            c0 = g * gw + pair * LANES
            xp = xdt[:, c0:c0 + LANES]
            zero = jnp.zeros_like(xp)
            rhs = jnp.concatenate([jnp.where(lo_half, xp, zero), jnp.where(lo_half, zero, xp)], axis=0)
            y_parts.append(_dot(jnp.concatenate(ms, axis=1), rhs))
    y = jnp.concatenate(y_parts, axis=1) + jnp.concatenate(y_offs, axis=1) * ein_x
    return y, o


def _scan_body(f_tile, f_out, f_head, f_first, b_tile, b_out, b_reset,
               actf_ref, xmeta_ref, xhead_ref, dtf_ref, qkvf_ref, vtf_ref, lff_ref,
               actb_ref, dtb_ref, qkvb_ref, vtb_ref, lfb_ref,
               cw_ref, cb_ref, arow_ref, ef_ref, eb_ref,
               yof_ref, yob_ref, sf_ssd, sf_hg, sb_ssd, sb_hg, meta_act):
    i = pl.program_id(0)
    first = f_first[i] == 1

    @pl.when(first)
    def _():
        sf_ssd[...] = jnp.zeros_like(sf_ssd)
        sf_hg[...] = jnp.zeros_like(sf_hg)
        meta_act[...] = _conv_silu(xmeta_ref[...].astype(F32), jnp.zeros((SUBLANES, SSD_CONV_DIM), F32),
                                   xhead_ref[...].astype(F32)[:SUBLANES], cw_ref[...], cb_ref[...]).astype(BF16)

    @pl.when(b_reset[i] == 1)
    def _():
        sb_ssd[...] = jnp.zeros_like(sb_ssd)
        sb_hg[...] = jnp.zeros_like(sb_hg)

    n_pad = TILE - N_META
    act = jnp.concatenate([actf_ref[:n_pad, :], jnp.where(first, meta_act[...], actf_ref[n_pad:, :])], axis=0)
    row = lax.broadcasted_iota(jnp.int32, (TILE, 1), 0)
    valid = jnp.where(jnp.logical_and(first, row < n_pad), 0.0, 1.0)
    dm, gn = D_MODEL, SSD_GN
    directions = [
        _mix_tile(act[:, :dm].astype(F32), act[:, dm:dm + gn], act[:, dm + gn:], dtf_ref[...] * valid,
                  qkvf_ref, vtf_ref, lff_ref, arow_ref, ef_ref, sf_ssd, sf_hg, bwd=False),
        _mix_tile(actb_ref[:, :dm].astype(F32), actb_ref[:, dm:dm + gn], actb_ref[:, dm + gn:], dtb_ref[...],
                  qkvb_ref, vtb_ref, lfb_ref, arow_ref, eb_ref, sb_ssd, sb_hg, bwd=True),
    ]
    results = [None] * len(directions)

    def advance(d):
        if results[d] is None:
            try:
                next(directions[d])
            except StopIteration as done:
                results[d] = done.value

    for _ in range(SCAN_LEAD):
        advance(0)
    while any(r is None for r in results):
        for d in range(len(directions)):
            advance(d)
    for (y, o), yo_ref in zip(results, (yof_ref, yob_ref)):
        yo_ref[:, :dm] = y.astype(BF16)
        yo_ref[:, dm:] = o.astype(BF16)


def _scan(tables, n_spare, meta_step, act, xmeta, xhead, dt, qkv, vt, lf3, cw, cb, arow, expands):
    t = act.shape[0]

    def fwd(width, col=0):
        return pl.BlockSpec((TILE, width), lambda i, f_tile, *_: (f_tile[i], col))

    def bwd(width, col=0):
        return pl.BlockSpec((TILE, width), lambda i, f_tile, f_out, f_head, f_first, b_tile, *_: (b_tile[i], col))

    vt_blk = (1, D_MODEL, TILE)
    vt_fwd = pl.BlockSpec(vt_blk, lambda i, f_tile, *_: (f_tile[i], 0, 0))
    vt_bwd = pl.BlockSpec(vt_blk, lambda i, f_tile, f_out, f_head, f_first, b_tile, *_: (b_tile[i], 0, 0))

    def const(shape):
        return pl.BlockSpec(shape, lambda i, *_: (0,) * len(shape))

    halo = (HALO_ROWS, SSD_CONV_DIM)
    in_specs = [
        fwd(SSD_CONV_DIM),
        pl.BlockSpec(halo, lambda i, *_: (meta_step, 0)),
        pl.BlockSpec(halo, lambda i, f_tile, f_out, f_head, *_: (f_head[i], 0)),
        fwd(LANES), fwd(4 * D_MODEL), vt_fwd, fwd(2 * D_MODEL, 0),
        bwd(SSD_CONV_DIM), bwd(LANES), bwd(4 * D_MODEL), vt_bwd, bwd(2 * D_MODEL, 1),
        const((SUBLANES, SSD_CONV_DIM)), const((1, SSD_CONV_DIM)), const((1, LANES)),
        const((LANES, D_MODEL)), const((LANES, D_MODEL)),
    ]
    out_specs = [
        pl.BlockSpec((TILE, 2 * D_MODEL), lambda i, f_tile, f_out, *_: (f_out[i], 0)),
        pl.BlockSpec((TILE, 2 * D_MODEL),
                     lambda i, f_tile, f_out, f_head, f_first, b_tile, b_out, *_: (b_out[i], 0)),
    ]
    state = [pltpu.VMEM((SSD_STATE, D_MODEL), F32), pltpu.VMEM((HG_HEADDIM, D_MODEL), F32)]
    grid_spec = pltpu.PrefetchScalarGridSpec(
        num_scalar_prefetch=len(tables), grid=(len(tables[0]),), in_specs=in_specs, out_specs=out_specs,
        scratch_shapes=state + state + [pltpu.VMEM((N_META, SSD_CONV_DIM), BF16)])
    yo = jax.ShapeDtypeStruct((t + n_spare * TILE, 2 * D_MODEL), BF16)
    return pl.pallas_call(
        _scan_body, grid_spec=grid_spec, out_shape=[yo, yo],
        compiler_params=_params("arbitrary"), name="scan",
    )(*tables, act, xmeta, xhead, dt, qkv, vt, lf3, act, dt, qkv, vt, lf3, cw, cb, arow, *expands)


def _merge_body(yof_ref, yob_ref, xs_ref, szsg_ref, gate_ref, h_ref, dexp_ref, snorm_ref, hnorm_ref,
                wa_ref, wb_ref, wo_ref, o_ref):
    dm = D_MODEL
    y = yof_ref[:, :dm].astype(F32) + yob_ref[:, :dm].astype(F32) + dexp_ref[...] * xs_ref[...].astype(F32)
    y = y * jax.nn.silu(szsg_ref[:, :dm].astype(F32))
    bra = _rms(y, snorm_ref[...]).astype(BF16)
    o_parts = []
    for h in range(HG_HEADS):
        cols = slice(dm + h * HG_HEADDIM, dm + (h + 1) * HG_HEADDIM)
        oh = yof_ref[:, cols].astype(F32) + yob_ref[:, cols].astype(F32)
        o_parts.append(oh * lax.rsqrt(jnp.mean(oh * oh, axis=-1, keepdims=True) + EPS))
    sg = jax.nn.silu(szsg_ref[:, dm:].astype(F32))
    brb = (jnp.concatenate(o_parts, axis=1) * hnorm_ref[...] * sg).astype(BF16)
    a = _dot(bra, wa_ref[...])
    b = _dot(brb, wb_ref[...])
    merged = (jax.nn.sigmoid(gate_ref[:, :dm].astype(F32)) * a
              + jax.nn.sigmoid(gate_ref[:, dm:].astype(F32)) * b)
    o_ref[...] = h_ref[...] + _dot(merged.astype(BF16), wo_ref[...])


def _merge(rows, yo_f, yo_b, act, szsg, gates, h, dexp, snorm, hnorm, wa, wb, wo):
    def row(width):
        return pl.BlockSpec((ROW_TILE, width), lambda i: (i, 0))

    sq = _resident((D_MODEL, D_MODEL))
    vec = _resident((1, D_MODEL))
    return pl.pallas_call(
        _merge_body,
        grid=(rows // ROW_TILE,),
        in_specs=[row(2 * D_MODEL), row(2 * D_MODEL), row(D_MODEL), row(2 * D_MODEL), row(2 * D_MODEL),
                  row(D_MODEL), vec, vec, vec, sq, sq, sq],
        out_specs=row(D_MODEL),
        out_shape=jax.ShapeDtypeStruct((rows, D_MODEL), F32),
        compiler_params=_params("parallel"),
        name="merge",
    )(yo_f, yo_b, act, szsg, gates, h, dexp, snorm, hnorm, wa, wb, wo)


def _scan_tables(seq_rows, meta_row, n_tiles):
    meta_tile = meta_row // TILE
    f_tile, f_out, f_head, f_first, b_tile, b_reset = [], [], [], [], [], []
    row0 = 0
    for seq, rows in enumerate(seq_rows):
        t0, n = row0 // TILE, rows // TILE
        f_tile += [meta_tile] + [t0 + j for j in range(n)]
        f_out += [n_tiles + seq] + [t0 + j for j in range(n)]
        f_head += [row0 // MERGE_TILE] * (n + 1)
        f_first += [1] + [0] * n
        b_tile += [t0 + j for j in reversed(range(n))]
        b_reset += [1] + [0] * (n - 1)
        row0 += rows
    n_seq, n_idle = len(seq_rows), len(f_tile) - len(b_tile)
    b_out = b_tile + [n_tiles + n_seq + j for j in range(n_idle)]
    b_tile = b_tile + [b_tile[-1]] * n_idle
    b_reset = b_reset + [0] * n_idle
    as_i32 = lambda xs: tuple(np.asarray(x, np.int32) for x in xs)
    return as_i32((f_tile, f_out, f_head, f_first, b_tile, b_out, b_reset)), n_seq + n_idle


def _inproj_tables(seq_rows, meta_row, n_steps):
    per_step = MERGE_TILE // SUBLANES
    prev, nxt, keep_prev, keep_next = [], [], [], []
    step = 0
    for rows in seq_rows:
        n = rows // MERGE_TILE
        for j in range(n):
            prev.append((meta_row + TILE) // SUBLANES - 1 if j == 0 else (step + j) * per_step - 1)
            nxt.append((step + j + 1) * per_step if j < n - 1 else 0)
            keep_prev.append(1)
            keep_next.append(1 if j < n - 1 else 0)
        step += n
    pad = n_steps - step
    as_i32 = lambda xs: tuple(np.asarray(x + [0] * pad, np.int32) for x in xs)
    return as_i32((prev, nxt, keep_prev, keep_next))


def kernel(x_prompt, x_sample, meta_tokens, ffn1_norm, ffn1_w_gate_up, ffn1_w_down, mix_norm, w_in, ssd_conv_w,
           ssd_conv_b, ssd_dt_bias, ssd_a_log, ssd_d, ssd_norm, ssd_w_proj, hg_lb_table, hg_norm, hg_w_proj, w_out,
           ffn2_norm, ffn2_w_gate_up, ffn2_w_down, final_norm):
    assert len(ffn1_norm) == 1, "single-layer block"
    groups = (x_prompt, x_sample)
    seq_rows = []
    for x in groups:
        b, s, dm = x.shape
        assert dm == D_MODEL and s % MERGE_TILE == 0 and (b * s) % ROW_TILE == 0
        seq_rows += [s] * b
    n_rows = sum(seq_rows)
    xp, xs = (x.reshape(-1, D_MODEL) for x in groups)
    tail = jnp.pad(meta_tokens.astype(F32), ((TILE - N_META, ROW_TILE - TILE), (0, 0)))
    scan_tables, n_spare = _scan_tables(seq_rows, n_rows, (n_rows + ROW_TILE) // TILE)

    row = lambda v: v.reshape(1, -1).astype(F32)

    h1 = _ffn_in(xp, xs, tail, row(ffn1_norm[0]), ffn1_w_gate_up[0].astype(BF16), ffn1_w_down[0].astype(BF16))

    pts = np.cumsum((0,) + IN_WIDTHS)
    w_f = w_in[0]
    dt_w = jnp.pad(w_f[:, pts[2]:pts[3]], ((0, 0), (0, LANES - 2 * SSD_HEADS)))
    w_packed = jnp.concatenate([w_f[:, :pts[2]], w_f[:, pts[3]:], dt_w], axis=1).astype(BF16)
    lb = jnp.cumsum(jax.nn.softmax(hg_lb_table.astype(F32), axis=1), axis=1)[:, 0].reshape(1, 2 * D_MODEL)
    dtb = jnp.pad(ssd_dt_bias[0].astype(F32).reshape(1, -1), ((0, 0), (0, LANES - 2 * SSD_HEADS)))
    cw = jnp.pad(ssd_conv_w[0].astype(F32), ((0, SUBLANES - SSD_CONV), (0, 0)))
    cb = row(ssd_conv_b[0])
    in_tables = _inproj_tables(seq_rows, n_rows, (n_rows + ROW_TILE) // MERGE_TILE)
    szsg, act, xhead, xmeta, qkv, vt, lf3, gates, dt = _inproj(in_tables, h1, row(mix_norm[0]), w_packed, lb, dtb,
                                                               cw, cb)

    arow = jnp.pad(-jnp.exp(ssd_a_log[0].astype(F32)).reshape(1, -1), ((0, 0), (0, LANES - 2 * SSD_HEADS)))
    head_of_col = np.arange(D_MODEL) // SSD_HEADDIM
    expands = [jnp.asarray((np.arange(LANES)[:, None] == head_of_col[None, :] + SSD_HEADS * d), BF16)
               for d in range(2)]
    dexp = jnp.repeat(ssd_d[0].astype(F32), SSD_HEADDIM).reshape(1, -1)

    yo_f, yo_b = _scan(scan_tables, n_spare, n_rows // MERGE_TILE, act, xmeta, xhead, dt, qkv, vt, lf3, cw, cb,
                       arow, expands)

    h2 = _merge(n_rows, yo_f, yo_b, act, szsg, gates, h1, dexp, row(ssd_norm[0]), row(hg_norm[0]),
                ssd_w_proj[0].astype(BF16), hg_w_proj[0].astype(BF16), w_out[0].astype(BF16))

    wgu2, wd2 = ffn2_w_gate_up[0].astype(BF16), ffn2_w_down[0].astype(BF16)
    outs, row0 = [], 0
    for x in groups:
        b, s, dm = x.shape
        y = _ffn_out(h2, row0, b * s, row(ffn2_norm[0]), wgu2, wd2, row(final_norm))
        outs.append(y.reshape(b, s, dm))
        row0 += b * s
    return tuple(outs)
```

```python
import functools

import numpy as np
import jax
import jax.numpy as jnp
from jax import lax
from jax.experimental import pallas as pl
from jax.experimental.pallas import tpu as pltpu

F32 = jnp.float32
BF16 = jnp.bfloat16

D_MODEL = 1024
N_META = 16
EPS = 1e-6
D_FF = 2816
SSD_HEADS = 16
SSD_HEADDIM = 64
SSD_GROUPS = 4
SSD_STATE = 128
SSD_CONV = 5
SSD_GN = SSD_GROUPS * SSD_STATE
SSD_CONV_DIM = D_MODEL + 2 * SSD_GN
HG_HEADS = 8
HG_HEADDIM = 128
HG_CHUNK = 64
IN_WIDTHS = (D_MODEL, SSD_CONV_DIM, 2 * SSD_HEADS, D_MODEL, 2 * D_MODEL, D_MODEL, D_MODEL, 2 * D_MODEL)

TILE = 128
ROW_TILE = 512
MERGE_TILE = 256
FFN_CHUNKS = 11
PROJ_CHUNK = 256
HALO_ROWS = 16
SUBLANES = 8
LANES = 128
VMEM_LIMIT = 56 * 1024 * 1024
LOG2E = 1.4426950408889634
SCAN_LEAD = 5


def _rms(x, w):
    return x * lax.rsqrt(jnp.mean(x * x, axis=-1, keepdims=True) + EPS) * w


def _dot(a, b):
    return jnp.dot(a, b, preferred_element_type=F32)


def _dot_nt(a, b):
    return lax.dot_general(a, b, (((1,), (1,)), ((), ())), preferred_element_type=F32)


def _dot_tn(a, b):
    return lax.dot_general(a, b, (((0,), (0,)), ((), ())), preferred_element_type=F32)


def _split3(x):
    x1 = x.astype(BF16)
    r1 = x - x1.astype(F32)
    x2 = r1.astype(BF16)
    x3 = (r1 - x2.astype(F32)).astype(BF16)
    return x1, x2, x3


def _dot01_rhs3(m01, x):
    x1, x2, x3 = _split3(x)
    return _dot(m01, x1) + _dot(m01, x2) + _dot(m01, x3)


def _dot01_lhs3(x, m01):
    x1, x2, x3 = _split3(x)
    return _dot(x1, m01) + _dot(x2, m01) + _dot(x3, m01)


def _resident(shape):
    return pl.BlockSpec(shape, lambda i, *_: (0,) * len(shape), pipeline_mode=pl.Buffered(1))


def _params(semantics):
    return pltpu.CompilerParams(dimension_semantics=(semantics,), vmem_limit_bytes=VMEM_LIMIT)


def _ffn_compute(x, nw_ref, wgu_ref, wd_ref, n_chunks):
    hn = _rms(x, nw_ref[...]).astype(BF16)
    tf = D_FF // n_chunks
    acc = jnp.zeros(x.shape, F32)
    for c in range(n_chunks):
        g = _dot(hn, wgu_ref[:, c * tf:(c + 1) * tf])
        u = _dot(hn, wgu_ref[:, D_FF + c * tf:D_FF + (c + 1) * tf])
        a = (jax.nn.silu(g) * u).astype(BF16)
        acc = acc + _dot(a, wd_ref[c * tf:(c + 1) * tf, :])
    return x + 0.5 * acc


def _ffn_in_body(xp_ref, xs_ref, tail_ref, nw_ref, wgu_ref, wd_ref, o_ref, *, steps_p, steps_s):
    i = pl.program_id(0)
    x = jnp.where(i < steps_p, xp_ref[...], jnp.where(i < steps_p + steps_s, xs_ref[...], tail_ref[...]))
    o_ref[...] = _ffn_compute(x, nw_ref, wgu_ref, wd_ref, FFN_CHUNKS)


def _ffn_in(xp, xs, tail, nw, wgu, wd):
    steps_p, steps_s = xp.shape[0] // ROW_TILE, xs.shape[0] // ROW_TILE
    steps = steps_p + steps_s + 1
    blk = (ROW_TILE, D_MODEL)
    return pl.pallas_call(
        functools.partial(_ffn_in_body, steps_p=steps_p, steps_s=steps_s),
        grid=(steps,),
        in_specs=[pl.BlockSpec(blk, lambda i: (jnp.minimum(i, steps_p - 1), 0)),
                  pl.BlockSpec(blk, lambda i: (jnp.clip(i - steps_p, 0, steps_s - 1), 0)),
                  pl.BlockSpec(blk, lambda i: (0, 0)),
                  _resident((1, D_MODEL)), _resident((D_MODEL, 2 * D_FF)), _resident((D_FF, D_MODEL))],
        out_specs=pl.BlockSpec(blk, lambda i: (i, 0)),
        out_shape=jax.ShapeDtypeStruct((steps * ROW_TILE, D_MODEL), F32),
        compiler_params=_params("parallel"),
        name="ffn_in",
    )(xp, xs, tail, nw, wgu, wd)


def _ffn_out_body(x_ref, nw_ref, wgu_ref, wd_ref, fnw_ref, o_ref):
    o_ref[...] = _rms(_ffn_compute(x_ref[...], nw_ref, wgu_ref, wd_ref, FFN_CHUNKS), fnw_ref[...])


def _ffn_out(h, row0, rows, nw, wgu, wd, fnw):
    blk = (ROW_TILE, D_MODEL)
    off = row0 // ROW_TILE
    return pl.pallas_call(
        _ffn_out_body,
        grid=(rows // ROW_TILE,),
        in_specs=[pl.BlockSpec(blk, lambda i: (i + off, 0)),
                  _resident((1, D_MODEL)), _resident((D_MODEL, 2 * D_FF)), _resident((D_FF, D_MODEL)),
                  _resident((1, D_MODEL))],
        out_specs=pl.BlockSpec(blk, lambda i: (i, 0)),
        out_shape=jax.ShapeDtypeStruct((rows, D_MODEL), F32),
        compiler_params=_params("parallel"),
        name="ffn_out",
    )(h, nw, wgu, wd, fnw)


_C_Z, _C_XBC, _C_Q, _C_F, _C_V, _C_G, _C_GATE, _C_DT, _C_END = (
    0, 1024, 3072, 4096, 6144, 7168, 8192, 10240, 10368)


def _inproj_body(prev_tbl, next_tbl, keep_prev_tbl, keep_next_tbl,
                 h_ref, hprev_ref, hnext_ref, nw_ref, w_ref, lb_ref, dtb_ref, cw_ref, cb_ref,
                 szsg_ref, act_ref, xhead_ref, xmeta_ref, qkv_ref, vt_ref, lf3_ref, gate_ref, dt_ref, xe_scr):
    i = pl.program_id(0)
    i_conv = jnp.maximum(i - 1, 0)
    rows = h_ref.shape[0]

    @pl.when(i == 0)
    def _():
        xe_scr[...] = jnp.zeros_like(xe_scr)
    h_ext = jnp.concatenate([h_ref[...], hprev_ref[...], hnext_ref[...]], axis=0)
    hn_ext = _rms(h_ext, nw_ref[...]).astype(BF16)
    hn = hn_ext[:rows]
    dm = D_MODEL

    cw = PROJ_CHUNK
    keep_prev = jnp.where(keep_prev_tbl[i_conv] == 1, 1.0, 0.0)
    keep_next = jnp.where(keep_next_tbl[i_conv] == 1, 1.0, 0.0)

    def proj(w0, c, lhs=hn):
        return _dot(lhs, w_ref[:, w0 + c:w0 + c + cw])

    def gated_chunks():
        for c in range(0, dm, cw):
            qkv_ref[:, c:c + cw] = jax.nn.silu(proj(_C_Q, c)).astype(BF16)
        for d in range(2):
            for c in range(0, dm, cw):
                lb = lb_ref[:, d * dm + c:d * dm + c + cw]
                f = lb + (1.0 - lb) * jax.nn.sigmoid(proj(_C_F + d * dm, c))
                qkv_ref[:, (1 + d) * dm + c:(1 + d) * dm + c + cw] = (1.0 - f).astype(BF16)
                lf = jnp.log(f)
                hi = lf.astype(BF16)
                lf3_ref[:, 2 * d * dm + c:2 * d * dm + c + cw] = hi
                lf3_ref[:, (2 * d + 1) * dm + c:(2 * d + 1) * dm + c + cw] = (lf - hi.astype(F32)).astype(BF16)

    def light_chunks():
        for c in range(0, dm, cw):
            szsg_ref[:, c:c + cw] = proj(_C_Z, c).astype(BF16)
            yield
        for c in range(0, dm, cw):
            v = proj(_C_V, c)
            qkv_ref[:, 3 * dm + c:3 * dm + c + cw] = v.astype(BF16)
            vt = v.T.astype(BF16)
            for j in range(rows // TILE):
                vt_ref[j, c:c + cw, :] = vt[:, j * TILE:(j + 1) * TILE]
            yield
        for c in range(0, dm, cw):
            szsg_ref[:, dm + c:dm + c + cw] = proj(_C_G, c).astype(BF16)
            yield
        for c in range(0, 2 * dm, cw):
            gate_ref[:, c:c + cw] = proj(_C_GATE, c).astype(BF16)
            yield

    light = light_chunks()
    n_conv = SSD_CONV_DIM // cw
    n_light = 5 * dm // cw
    for k in range(n_conv):
        c = k * cw
        old = xe_scr[:, c:c + cw]
        xe = proj(_C_XBC, c, hn_ext)
        xe_scr[:, c:c + cw] = xe
        xhead_ref[:, c:c + cw] = xe[:HALO_ROWS].astype(BF16)
        xmeta_ref[:, c:c + cw] = xe[TILE - N_META:TILE].astype(BF16)
        for _ in range((k + 1) * n_light // n_conv - k * n_light // n_conv):
            next(light)
        act = _conv_silu(old[:rows], old[rows:rows + SUBLANES] * keep_prev, old[rows + SUBLANES:] * keep_next,
                         cw_ref[:, c:c + cw], cb_ref[:, c:c + cw])
        act_ref[:, c:c + cw] = act.astype(BF16)
    gated_chunks()
    dt_ref[...] = jax.nn.softplus(_dot(hn, w_ref[:, _C_DT:_C_END]) + dtb_ref[...])


def _inproj(tables, h, nw, w, lb, dtb, cw, cb):
    t = h.shape[0]
    steps = t // MERGE_TILE

    def tile(i):
        return jnp.minimum(i, steps - 1)

    def row(width):
        return pl.BlockSpec((MERGE_TILE, width), lambda i, *_: (tile(i), 0))

    def head(width):
        return pl.BlockSpec((HALO_ROWS, width), lambda i, *_: (tile(i), 0))

    halo = (SUBLANES, D_MODEL)
    in_specs = [row(D_MODEL),
                pl.BlockSpec(halo, lambda i, prev, nxt, *_: (prev[tile(i)], 0)),
                pl.BlockSpec(halo, lambda i, prev, nxt, *_: (nxt[tile(i)], 0)),
                _resident((1, D_MODEL)), _resident((D_MODEL, _C_END)), _resident((1, 2 * D_MODEL)),
                _resident((1, LANES)), _resident((SUBLANES, SSD_CONV_DIM)), _resident((1, SSD_CONV_DIM))]
    out_specs = [row(2 * D_MODEL),
                 pl.BlockSpec((MERGE_TILE, SSD_CONV_DIM), lambda i, *_: (jnp.maximum(i - 1, 0), 0)),
                 head(SSD_CONV_DIM), head(SSD_CONV_DIM), row(4 * D_MODEL),
                 pl.BlockSpec((MERGE_TILE // TILE, D_MODEL, TILE), lambda i, *_: (tile(i), 0, 0)),
                 row(4 * D_MODEL), row(2 * D_MODEL), row(LANES)]
    out_shape = [jax.ShapeDtypeStruct((t, 2 * D_MODEL), BF16), jax.ShapeDtypeStruct((t, SSD_CONV_DIM), BF16),
                 jax.ShapeDtypeStruct((steps * HALO_ROWS, SSD_CONV_DIM), BF16),
                 jax.ShapeDtypeStruct((steps * HALO_ROWS, SSD_CONV_DIM), BF16),
                 jax.ShapeDtypeStruct((t, 4 * D_MODEL), BF16), jax.ShapeDtypeStruct((t // TILE, D_MODEL, TILE), BF16),
                 jax.ShapeDtypeStruct((t, 4 * D_MODEL), BF16),
                 jax.ShapeDtypeStruct((t, 2 * D_MODEL), BF16), jax.ShapeDtypeStruct((t, LANES), F32)]
    grid_spec = pltpu.PrefetchScalarGridSpec(
        num_scalar_prefetch=len(tables), grid=(steps + 1,), in_specs=in_specs, out_specs=out_specs,
        scratch_shapes=[pltpu.VMEM((MERGE_TILE + 2 * SUBLANES, SSD_CONV_DIM), F32)])
    return pl.pallas_call(
        _inproj_body, grid_spec=grid_spec, out_shape=out_shape,
        compiler_params=_params("arbitrary"), name="inproj",
    )(*tables, h, h, h, nw, w, lb, dtb, cw, cb)


def _conv_silu(x, prev8, next8, cw, cb):
    half = SSD_CONV // 2
    sub = lax.broadcasted_iota(jnp.int32, (SUBLANES, 1), 0)
    acc = cb + cw[half:half + 1, :] * x
    for j in range(SSD_CONV):
        sh = half - j
        if sh == 0:
            continue
        rolled = pltpu.roll(x, sh % x.shape[0], axis=0)
        if sh > 0:
            fix = jnp.where(sub < sh, pltpu.roll(prev8, sh, axis=0), rolled[:SUBLANES])
            rolled = jnp.concatenate([fix, rolled[SUBLANES:]], axis=0)
        else:
            fix = jnp.where(sub >= SUBLANES + sh, pltpu.roll(next8, SUBLANES + sh, axis=0), rolled[-SUBLANES:])
            rolled = jnp.concatenate([rolled[:-SUBLANES], fix], axis=0)
        acc = acc + cw[j:j + 1, :] * rolled
    return jax.nn.silu(acc)


def _mix_tile(xs, bm, cm, dtv, qkv_ref, vt_ref, lf3_ref, arow_ref, e_ref, s_ssd, s_hg, *, bwd):
    hpg = SSD_HEADS // SSD_GROUPS
    gw = hpg * SSD_HEADDIM
    cq = HG_CHUNK
    n_chunks = TILE // cq
    chunk_order = list(reversed(range(n_chunks))) if bwd else list(range(n_chunks))
    g_cols = [slice(g * gw, (g + 1) * gw) for g in range(SSD_GROUPS)]
    g_state = [slice(g * SSD_STATE, (g + 1) * SSD_STATE) for g in range(SSD_GROUPS)]
    h_cols = [slice(h * HG_HEADDIM, (h + 1) * HG_HEADDIM) for h in range(HG_HEADS)]

    r_i = lax.broadcasted_iota(jnp.int32, (TILE, TILE), 0)
    c_i = lax.broadcasted_iota(jnp.int32, (TILE, TILE), 1)
    causal = (r_i <= c_i) if bwd else (r_i >= c_i)
    tri = jnp.where(causal, 1.0, 0.0).astype(BF16)
    causal64 = causal[:cq, :cq]
    tri_chunks = jnp.where(jnp.logical_and(causal, r_i // cq == c_i // cq), 1.0, 0.0).astype(BF16)
    dm = D_MODEL
    k0 = (2 if bwd else 1) * dm
    q_all = qkv_ref[:, :dm]
    k_all = qkv_ref[:, k0:k0 + dm]
    v_all = qkv_ref[:, 3 * dm:]
    end_row = 0 if bwd else TILE - 1
    ref_row = (cq - 1 - cq // 2) if bwd else cq // 2
    end64 = 0 if bwd else cq - 1

    s_old = s_ssd[...]
    s_old_b = s_old.astype(BF16)
    cbs = [_dot_nt(cm[:, g_state[g]], bm[:, g_state[g]]) for g in range(SSD_GROUPS)]
    y_offs = [_dot(cm[:, g_state[g]], s_old_b[:, g_cols[g]]) for g in range(SSD_GROUPS)]
    acs = _dot01_rhs3(tri, dtv * arow_ref[...]) * LOG2E
    bcs_tile = (_dot(tri_chunks, lf3_ref[:, :dm]) + _dot(tri_chunks, lf3_ref[:, dm:])) * LOG2E
    bcs_all = [bcs_tile[ci * cq:(ci + 1) * cq] for ci in range(n_chunks)]
    yield

    total = acs[end_row:end_row + 1, :]
    e_in = jnp.exp2(acs)
    e_end = jnp.exp2(total - acs)
    expand = e_ref[...]
    dt_x = _dot(dtv.astype(BF16), expand)
    w_x = _dot((dtv * e_end).astype(BF16), expand)
    ein_x = _dot(e_in.astype(BF16), expand)
    etot_x = _dot01_lhs3(jnp.broadcast_to(jnp.exp2(total), (SUBLANES, LANES)), expand)[0:1, :]
    acs_t = acs.T
    yield

    ca, cb_ = chunk_order
    hg = {}
    for ci in chunk_order:
        rows = slice(ci * cq, (ci + 1) * cq)
        qc, kc, bcs = q_all[rows], k_all[rows], bcs_all[ci]
        bref = bcs[ref_row:ref_row + 1, :]
        btot = bcs[end64:end64 + 1, :]
        qe = qc * jnp.exp2(bcs - bref).astype(BF16)
        ke = kc * jnp.exp2(bref - bcs).astype(BF16)
        kd = kc * jnp.exp2(btot - bcs).astype(BF16)
        qd = qc * jnp.exp2(bcs).astype(BF16)
        scores = [_dot_nt(qe[:, c], ke[:, c]) for c in h_cols]
        hg[ci] = (scores, kd, qd, jnp.exp2(btot), v_all[rows])
        yield
    scores_a, kd_a, qd_a, decay_a, v_a = hg[ca]
    scores_b, kd_b, qd_b, decay_b, v_b = hg[cb_]
    cross = [_dot_nt(qd_b[:, c], kd_a[:, c]) for c in h_cols]
    in_order = (lambda a, b: [a, b]) if ca < cb_ else (lambda a, b: [b, a])
    qd_tile = jnp.concatenate(in_order(qd_a, qd_b * decay_a.astype(BF16)), axis=0)
    kd_tile = jnp.concatenate(in_order(kd_a * decay_b.astype(BF16), kd_b), axis=0)
    yield

    xdt = (xs * dt_x).astype(BF16)
    xdtd = (xs * w_x).astype(BF16)
    for g in range(SSD_GROUPS):
        s_ssd[:, g_cols[g]] = (s_old[:, g_cols[g]] * etot_x[:, g_cols[g]]
                               + _dot_tn(bm[:, g_state[g]], xdtd[:, g_cols[g]]))
    yield

    st = s_hg[...]
    st_b = st.astype(BF16)
    inter = [_dot_nt(qd_tile[:, c], st_b[:, c]) for c in h_cols]
    upd = [_dot(vt_ref[0, c, :], kd_tile[:, c]) for c in h_cols]
    s_hg[...] = st * (decay_a * decay_b) + jnp.concatenate(upd, axis=1)
    yield
    o_parts = []
    for h, c in enumerate(h_cols):
        intra_a = _dot(jnp.where(causal64, scores_a[h], 0.0).astype(BF16), v_a[:, c])
        intra_b = (_dot(jnp.where(causal64, scores_b[h], 0.0).astype(BF16), v_b[:, c])
                   + _dot(cross[h].astype(BF16), v_a[:, c]))
        o_parts.append(jnp.concatenate(in_order(intra_a, intra_b), axis=0) + inter[h])
    o = jnp.concatenate(o_parts, axis=1)
    yield

    lane_i = lax.broadcasted_iota(jnp.int32, (TILE, LANES), 1)
    lo_half = lane_i < SSD_HEADDIM
    neg_inf = jnp.float32(-jnp.inf)
    y_parts = []
    for g in range(SSD_GROUPS):
        for pair in range(hpg // 2):
            ms = []
            for hh in range(2):
                lane = (SSD_HEADS if bwd else 0) + g * hpg + pair * 2 + hh
                seg = acs[:, lane:lane + 1] - acs_t[lane:lane + 1, :]
                ms.append((cbs[g] * jnp.exp2(jnp.where(causal, seg, neg_inf))).astype(BF16))
            c0 = g * gw + pair * LANES
            xp = xdt[:, c0:c0 + LANES]
            zero = jnp.zeros_like(xp)
            rhs = jnp.concatenate([jnp.where(lo_half, xp, zero), jnp.where(lo_half, zero, xp)], axis=0)
            y_parts.append(_dot(jnp.concatenate(ms, axis=1), rhs))
    y = jnp.concatenate(y_parts, axis=1) + jnp.concatenate(y_offs, axis=1) * ein_x
    return y, o


def _scan_body(f_tile, f_out, f_head, f_first, b_tile, b_out, b_reset,
               actf_ref, xmeta_ref, xhead_ref, dtf_ref, qkvf_ref, vtf_ref, lff_ref,
               actb_ref, dtb_ref, qkvb_ref, vtb_ref, lfb_ref,
               cw_ref, cb_ref, arow_ref, ef_ref, eb_ref,
               yof_ref, yob_ref, sf_ssd, sf_hg, sb_ssd, sb_hg, meta_act):
    i = pl.program_id(0)
    first = f_first[i] == 1

    @pl.when(first)
    def _():
        sf_ssd[...] = jnp.zeros_like(sf_ssd)
        sf_hg[...] = jnp.zeros_like(sf_hg)
        meta_act[...] = _conv_silu(xmeta_ref[...].astype(F32), jnp.zeros((SUBLANES, SSD_CONV_DIM), F32),
                                   xhead_ref[...].astype(F32)[:SUBLANES], cw_ref[...], cb_ref[...]).astype(BF16)

    @pl.when(b_reset[i] == 1)
    def _():
        sb_ssd[...] = jnp.zeros_like(sb_ssd)
        sb_hg[...] = jnp.zeros_like(sb_hg)

    n_pad = TILE - N_META
    act = jnp.concatenate([actf_ref[:n_pad, :], jnp.where(first, meta_act[...], actf_ref[n_pad:, :])], axis=0)
    row = lax.broadcasted_iota(jnp.int32, (TILE, 1), 0)
    valid = jnp.where(jnp.logical_and(first, row < n_pad), 0.0, 1.0)
    dm, gn = D_MODEL, SSD_GN
    directions = [
        _mix_tile(act[:, :dm].astype(F32), act[:, dm:dm + gn], act[:, dm + gn:], dtf_ref[...] * valid,
                  qkvf_ref, vtf_ref, lff_ref, arow_ref, ef_ref, sf_ssd, sf_hg, bwd=False),
        _mix_tile(actb_ref[:, :dm].astype(F32), actb_ref[:, dm:dm + gn], actb_ref[:, dm + gn:], dtb_ref[...],
                  qkvb_ref, vtb_ref, lfb_ref, arow_ref, eb_ref, sb_ssd, sb_hg, bwd=True),
    ]
    results = [None] * len(directions)

    def advance(d):
        if results[d] is None:
            try:
                next(directions[d])
            except StopIteration as done:
                results[d] = done.value

    for _ in range(SCAN_LEAD):
        advance(0)
    while any(r is None for r in results):
        for d in range(len(directions)):
            advance(d)
    for (y, o), yo_ref in zip(results, (yof_ref, yob_ref)):
        yo_ref[:, :dm] = y.astype(BF16)
        yo_ref[:, dm:] = o.astype(BF16)


def _scan(tables, n_spare, meta_step, act, xmeta, xhead, dt, qkv, vt, lf3, cw, cb, arow, expands):
    t = act.shape[0]

    def fwd(width, col=0):
        return pl.BlockSpec((TILE, width), lambda i, f_tile, *_: (f_tile[i], col))

    def bwd(width, col=0):
        return pl.BlockSpec((TILE, width), lambda i, f_tile, f_out, f_head, f_first, b_tile, *_: (b_tile[i], col))

    vt_blk = (1, D_MODEL, TILE)
    vt_fwd = pl.BlockSpec(vt_blk, lambda i, f_tile, *_: (f_tile[i], 0, 0))
    vt_bwd = pl.BlockSpec(vt_blk, lambda i, f_tile, f_out, f_head, f_first, b_tile, *_: (b_tile[i], 0, 0))

    def const(shape):
        return pl.BlockSpec(shape, lambda i, *_: (0,) * len(shape))

    halo = (HALO_ROWS, SSD_CONV_DIM)
    in_specs = [
        fwd(SSD_CONV_DIM),
        pl.BlockSpec(halo, lambda i, *_: (meta_step, 0)),
        pl.BlockSpec(halo, lambda i, f_tile, f_out, f_head, *_: (f_head[i], 0)),
        fwd(LANES), fwd(4 * D_MODEL), vt_fwd, fwd(2 * D_MODEL, 0),
        bwd(SSD_CONV_DIM), bwd(LANES), bwd(4 * D_MODEL), vt_bwd, bwd(2 * D_MODEL, 1),
        const((SUBLANES, SSD_CONV_DIM)), const((1, SSD_CONV_DIM)), const((1, LANES)),
        const((LANES, D_MODEL)), const((LANES, D_MODEL)),
    ]
    out_specs = [
        pl.BlockSpec((TILE, 2 * D_MODEL), lambda i, f_tile, f_out, *_: (f_out[i], 0)),
        pl.BlockSpec((TILE, 2 * D_MODEL),
                     lambda i, f_tile, f_out, f_head, f_first, b_tile, b_out, *_: (b_out[i], 0)),
    ]
    state = [pltpu.VMEM((SSD_STATE, D_MODEL), F32), pltpu.VMEM((HG_HEADDIM, D_MODEL), F32)]
    grid_spec = pltpu.PrefetchScalarGridSpec(
        num_scalar_prefetch=len(tables), grid=(len(tables[0]),), in_specs=in_specs, out_specs=out_specs,
        scratch_shapes=state + state + [pltpu.VMEM((N_META, SSD_CONV_DIM), BF16)])
    yo = jax.ShapeDtypeStruct((t + n_spare * TILE, 2 * D_MODEL), BF16)
    return pl.pallas_call(
        _scan_body, grid_spec=grid_spec, out_shape=[yo, yo],
        compiler_params=_params("arbitrary"), name="scan",
    )(*tables, act, xmeta, xhead, dt, qkv, vt, lf3, act, dt, qkv, vt, lf3, cw, cb, arow, *expands)


def _merge_body(yof_ref, yob_ref, xs_ref, szsg_ref, gate_ref, h_ref, dexp_ref, snorm_ref, hnorm_ref,
                wa_ref, wb_ref, wo_ref, o_ref):
    dm = D_MODEL
    y = yof_ref[:, :dm].astype(F32) + yob_ref[:, :dm].astype(F32) + dexp_ref[...] * xs_ref[...].astype(F32)
    y = y * jax.nn.silu(szsg_ref[:, :dm].astype(F32))
    bra = _rms(y, snorm_ref[...]).astype(BF16)
    o_parts = []
    for h in range(HG_HEADS):
        cols = slice(dm + h * HG_HEADDIM, dm + (h + 1) * HG_HEADDIM)
        oh = yof_ref[:, cols].astype(F32) + yob_ref[:, cols].astype(F32)
        o_parts.append(oh * lax.rsqrt(jnp.mean(oh * oh, axis=-1, keepdims=True) + EPS))
    sg = jax.nn.silu(szsg_ref[:, dm:].astype(F32))
    brb = (jnp.concatenate(o_parts, axis=1) * hnorm_ref[...] * sg).astype(BF16)
    a = _dot(bra, wa_ref[...])
    b = _dot(brb, wb_ref[...])
    merged = (jax.nn.sigmoid(gate_ref[:, :dm].astype(F32)) * a
              + jax.nn.sigmoid(gate_ref[:, dm:].astype(F32)) * b)
    o_ref[...] = h_ref[...] + _dot(merged.astype(BF16), wo_ref[...])


def _merge(rows, yo_f, yo_b, act, szsg, gates, h, dexp, snorm, hnorm, wa, wb, wo):
    def row(width):
        return pl.BlockSpec((ROW_TILE, width), lambda i: (i, 0))

    sq = _resident((D_MODEL, D_MODEL))
    vec = _resident((1, D_MODEL))
    return pl.pallas_call(
        _merge_body,
        grid=(rows // ROW_TILE,),
        in_specs=[row(2 * D_MODEL), row(2 * D_MODEL), row(D_MODEL), row(2 * D_MODEL), row(2 * D_MODEL),
                  row(D_MODEL), vec, vec, vec, sq, sq, sq],
        out_specs=row(D_MODEL),
        out_shape=jax.ShapeDtypeStruct((rows, D_MODEL), F32),
        compiler_params=_params("parallel"),
        name="merge",
    )(yo_f, yo_b, act, szsg, gates, h, dexp, snorm, hnorm, wa, wb, wo)


def _scan_tables(seq_rows, meta_row, n_tiles):
    meta_tile = meta_row // TILE
    f_tile, f_out, f_head, f_first, b_tile, b_reset = [], [], [], [], [], []
    row0 = 0
    for seq, rows in enumerate(seq_rows):
        t0, n = row0 // TILE, rows // TILE
        f_tile += [meta_tile] + [t0 + j for j in range(n)]
        f_out += [n_tiles + seq] + [t0 + j for j in range(n)]
        f_head += [row0 // MERGE_TILE] * (n + 1)
        f_first += [1] + [0] * n
        b_tile += [t0 + j for j in reversed(range(n))]
        b_reset += [1] + [0] * (n - 1)
        row0 += rows
    n_seq, n_idle = len(seq_rows), len(f_tile) - len(b_tile)
    b_out = b_tile + [n_tiles + n_seq + j for j in range(n_idle)]
    b_tile = b_tile + [b_tile[-1]] * n_idle
    b_reset = b_reset + [0] * n_idle
    as_i32 = lambda xs: tuple(np.asarray(x, np.int32) for x in xs)
    return as_i32((f_tile, f_out, f_head, f_first, b_tile, b_out, b_reset)), n_seq + n_idle


def _inproj_tables(seq_rows, meta_row, n_steps):
    per_step = MERGE_TILE // SUBLANES
    prev, nxt, keep_prev, keep_next = [], [], [], []
    step = 0
    for rows in seq_rows:
        n = rows // MERGE_TILE
        for j in range(n):
            prev.append((meta_row + TILE) // SUBLANES - 1 if j == 0 else (step + j) * per_step - 1)
            nxt.append((step + j + 1) * per_step if j < n - 1 else 0)
            keep_prev.append(1)
            keep_next.append(1 if j < n - 1 else 0)
        step += n
    pad = n_steps - step
    as_i32 = lambda xs: tuple(np.asarray(x + [0] * pad, np.int32) for x in xs)
    return as_i32((prev, nxt, keep_prev, keep_next))


def kernel(x_prompt, x_sample, meta_tokens, ffn1_norm, ffn1_w_gate_up, ffn1_w_down, mix_norm, w_in, ssd_conv_w,
           ssd_conv_b, ssd_dt_bias, ssd_a_log, ssd_d, ssd_norm, ssd_w_proj, hg_lb_table, hg_norm, hg_w_proj, w_out,
           ffn2_norm, ffn2_w_gate_up, ffn2_w_down, final_norm):
    assert len(ffn1_norm) == 1, "single-layer block"
    groups = (x_prompt, x_sample)
    seq_rows = []
    for x in groups:
        b, s, dm = x.shape
        assert dm == D_MODEL and s % MERGE_TILE == 0 and (b * s) % ROW_TILE == 0
        seq_rows += [s] * b
    n_rows = sum(seq_rows)
    xp, xs = (x.reshape(-1, D_MODEL) for x in groups)
    tail = jnp.pad(meta_tokens.astype(F32), ((TILE - N_META, ROW_TILE - TILE), (0, 0)))
    scan_tables, n_spare = _scan_tables(seq_rows, n_rows, (n_rows + ROW_TILE) // TILE)

    row = lambda v: v.reshape(1, -1).astype(F32)

    h1 = _ffn_in(xp, xs, tail, row(ffn1_norm[0]), ffn1_w_gate_up[0].astype(BF16), ffn1_w_down[0].astype(BF16))

    pts = np.cumsum((0,) + IN_WIDTHS)
    w_f = w_in[0]
    dt_w = jnp.pad(w_f[:, pts[2]:pts[3]], ((0, 0), (0, LANES - 2 * SSD_HEADS)))
    w_packed = jnp.concatenate([w_f[:, :pts[2]], w_f[:, pts[3]:], dt_w], axis=1).astype(BF16)
    lb = jnp.cumsum(jax.nn.softmax(hg_lb_table.astype(F32), axis=1), axis=1)[:, 0].reshape(1, 2 * D_MODEL)
    dtb = jnp.pad(ssd_dt_bias[0].astype(F32).reshape(1, -1), ((0, 0), (0, LANES - 2 * SSD_HEADS)))
    cw = jnp.pad(ssd_conv_w[0].astype(F32), ((0, SUBLANES - SSD_CONV), (0, 0)))
    cb = row(ssd_conv_b[0])
    in_tables = _inproj_tables(seq_rows, n_rows, (n_rows + ROW_TILE) // MERGE_TILE)
    szsg, act, xhead, xmeta, qkv, vt, lf3, gates, dt = _inproj(in_tables, h1, row(mix_norm[0]), w_packed, lb, dtb,
                                                               cw, cb)

    arow = jnp.pad(-jnp.exp(ssd_a_log[0].astype(F32)).reshape(1, -1), ((0, 0), (0, LANES - 2 * SSD_HEADS)))
    head_of_col = np.arange(D_MODEL) // SSD_HEADDIM
    expands = [jnp.asarray((np.arange(LANES)[:, None] == head_of_col[None, :] + SSD_HEADS * d), BF16)
               for d in range(2)]
    dexp = jnp.repeat(ssd_d[0].astype(F32), SSD_HEADDIM).reshape(1, -1)

    yo_f, yo_b = _scan(scan_tables, n_spare, n_rows // MERGE_TILE, act, xmeta, xhead, dt, qkv, vt, lf3, cw, cb,
                       arow, expands)

    h2 = _merge(n_rows, yo_f, yo_b, act, szsg, gates, h1, dexp, row(ssd_norm[0]), row(hg_norm[0]),
                ssd_w_proj[0].astype(BF16), hg_w_proj[0].astype(BF16), w_out[0].astype(BF16))

    wgu2, wd2 = ffn2_w_gate_up[0].astype(BF16), ffn2_w_down[0].astype(BF16)
    outs, row0 = [], 0
    for x in groups:
        b, s, dm = x.shape
        y = _ffn_out(h2, row0, b * s, row(ffn2_norm[0]), wgu2, wd2, row(final_norm))
        outs.append(y.reshape(b, s, dm))
        row0 += b * s
    return tuple(outs)
```

```python
import functools

import numpy as np
import jax
import jax.numpy as jnp
from jax import lax
from jax.experimental import pallas as pl
from jax.experimental.pallas import tpu as pltpu

F32 = jnp.float32
BF16 = jnp.bfloat16

D_MODEL = 1024
N_META = 16
EPS = 1e-6
D_FF = 2816
SSD_HEADS = 16
SSD_HEADDIM = 64
SSD_GROUPS = 4
SSD_STATE = 128
SSD_CONV = 5
SSD_GN = SSD_GROUPS * SSD_STATE
SSD_CONV_DIM = D_MODEL + 2 * SSD_GN
HG_HEADS = 8
HG_HEADDIM = 128
HG_CHUNK = 64
IN_WIDTHS = (D_MODEL, SSD_CONV_DIM, 2 * SSD_HEADS, D_MODEL, 2 * D_MODEL, D_MODEL, D_MODEL, 2 * D_MODEL)

TILE = 128
ROW_TILE = 512
MERGE_TILE = 256
FFN_CHUNKS = 11
PROJ_CHUNK = 256
HALO_ROWS = 16
SUBLANES = 8
LANES = 128
VMEM_LIMIT = 56 * 1024 * 1024
LOG2E = 1.4426950408889634
SCAN_LEAD = 5


def _rms(x, w):
    return x * lax.rsqrt(jnp.mean(x * x, axis=-1, keepdims=True) + EPS) * w


def _dot(a, b):
    return jnp.dot(a, b, preferred_element_type=F32)


def _dot_nt(a, b):
    return lax.dot_general(a, b, (((1,), (1,)), ((), ())), preferred_element_type=F32)


def _dot_tn(a, b):
    return lax.dot_general(a, b, (((0,), (0,)), ((), ())), preferred_element_type=F32)


def _split3(x):
    x1 = x.astype(BF16)
    r1 = x - x1.astype(F32)
    x2 = r1.astype(BF16)
    x3 = (r1 - x2.astype(F32)).astype(BF16)
    return x1, x2, x3


def _dot01_rhs3(m01, x):
    x1, x2, x3 = _split3(x)
    return _dot(m01, x1) + _dot(m01, x2) + _dot(m01, x3)


def _dot01_lhs3(x, m01):
    x1, x2, x3 = _split3(x)
    return _dot(x1, m01) + _dot(x2, m01) + _dot(x3, m01)


def _resident(shape):
    return pl.BlockSpec(shape, lambda i, *_: (0,) * len(shape), pipeline_mode=pl.Buffered(1))


def _params(semantics):
    return pltpu.CompilerParams(dimension_semantics=(semantics,), vmem_limit_bytes=VMEM_LIMIT)


def _ffn_compute(x, nw_ref, wgu_ref, wd_ref, n_chunks):
    hn = _rms(x, nw_ref[...]).astype(BF16)
    tf = D_FF // n_chunks
    acc = jnp.zeros(x.shape, F32)
    for c in range(n_chunks):
        g = _dot(hn, wgu_ref[:, c * tf:(c + 1) * tf])
        u = _dot(hn, wgu_ref[:, D_FF + c * tf:D_FF + (c + 1) * tf])
        a = (jax.nn.silu(g) * u).astype(BF16)
        acc = acc + _dot(a, wd_ref[c * tf:(c + 1) * tf, :])
    return x + 0.5 * acc


def _ffn_in_body(xp_ref, xs_ref, tail_ref, nw_ref, wgu_ref, wd_ref, o_ref, *, steps_p, steps_s):
    i = pl.program_id(0)
    x = jnp.where(i < steps_p, xp_ref[...], jnp.where(i < steps_p + steps_s, xs_ref[...], tail_ref[...]))
    o_ref[...] = _ffn_compute(x, nw_ref, wgu_ref, wd_ref, FFN_CHUNKS)


def _ffn_in(xp, xs, tail, nw, wgu, wd):
    steps_p, steps_s = xp.shape[0] // ROW_TILE, xs.shape[0] // ROW_TILE
    steps = steps_p + steps_s + 1
    blk = (ROW_TILE, D_MODEL)
    return pl.pallas_call(
        functools.partial(_ffn_in_body, steps_p=steps_p, steps_s=steps_s),
        grid=(steps,),
        in_specs=[pl.BlockSpec(blk, lambda i: (jnp.minimum(i, steps_p - 1), 0)),
                  pl.BlockSpec(blk, lambda i: (jnp.clip(i - steps_p, 0, steps_s - 1), 0)),
                  pl.BlockSpec(blk, lambda i: (0, 0)),
                  _resident((1, D_MODEL)), _resident((D_MODEL, 2 * D_FF)), _resident((D_FF, D_MODEL))],
        out_specs=pl.BlockSpec(blk, lambda i: (i, 0)),
        out_shape=jax.ShapeDtypeStruct((steps * ROW_TILE, D_MODEL), F32),
        compiler_params=_params("parallel"),
        name="ffn_in",
    )(xp, xs, tail, nw, wgu, wd)


def _ffn_out_body(x_ref, nw_ref, wgu_ref, wd_ref, fnw_ref, o_ref):
    o_ref[...] = _rms(_ffn_compute(x_ref[...], nw_ref, wgu_ref, wd_ref, FFN_CHUNKS), fnw_ref[...])


def _ffn_out(h, row0, rows, nw, wgu, wd, fnw):
    blk = (ROW_TILE, D_MODEL)
    off = row0 // ROW_TILE
    return pl.pallas_call(
        _ffn_out_body,
        grid=(rows // ROW_TILE,),
        in_specs=[pl.BlockSpec(blk, lambda i: (i + off, 0)),
                  _resident((1, D_MODEL)), _resident((D_MODEL, 2 * D_FF)), _resident((D_FF, D_MODEL)),
                  _resident((1, D_MODEL))],
        out_specs=pl.BlockSpec(blk, lambda i: (i, 0)),
        out_shape=jax.ShapeDtypeStruct((rows, D_MODEL), F32),
        compiler_params=_params("parallel"),
        name="ffn_out",
    )(h, nw, wgu, wd, fnw)


_C_Z, _C_XBC, _C_Q, _C_F, _C_V, _C_G, _C_GATE, _C_DT, _C_END = (
    0, 1024, 3072, 4096, 6144, 7168, 8192, 10240, 10368)


def _inproj_body(prev_tbl, next_tbl, keep_prev_tbl, keep_next_tbl,
                 h_ref, hprev_ref, hnext_ref, nw_ref, w_ref, lb_ref, dtb_ref, cw_ref, cb_ref,
                 szsg_ref, act_ref, xhead_ref, xmeta_ref, qkv_ref, vt_ref, lf3_ref, gate_ref, dt_ref, xe_scr):
    i = pl.program_id(0)
    i_conv = jnp.maximum(i - 1, 0)
    rows = h_ref.shape[0]

    @pl.when(i == 0)
    def _():
        xe_scr[...] = jnp.zeros_like(xe_scr)
    h_ext = jnp.concatenate([h_ref[...], hprev_ref[...], hnext_ref[...]], axis=0)
    hn_ext = _rms(h_ext, nw_ref[...]).astype(BF16)
    hn = hn_ext[:rows]
    dm = D_MODEL

    cw = PROJ_CHUNK
    keep_prev = jnp.where(keep_prev_tbl[i_conv] == 1, 1.0, 0.0)
    keep_next = jnp.where(keep_next_tbl[i_conv] == 1, 1.0, 0.0)

    def proj(w0, c, lhs=hn):
        return _dot(lhs, w_ref[:, w0 + c:w0 + c + cw])

    def gated_chunks():
        for c in range(0, dm, cw):
            qkv_ref[:, c:c + cw] = jax.nn.silu(proj(_C_Q, c)).astype(BF16)
        for d in range(2):
            for c in range(0, dm, cw):
                lb = lb_ref[:, d * dm + c:d * dm + c + cw]
                f = lb + (1.0 - lb) * jax.nn.sigmoid(proj(_C_F + d * dm, c))
                qkv_ref[:, (1 + d) * dm + c:(1 + d) * dm + c + cw] = (1.0 - f).astype(BF16)
                lf = jnp.log(f)
                hi = lf.astype(BF16)
                lf3_ref[:, 2 * d * dm + c:2 * d * dm + c + cw] = hi
                lf3_ref[:, (2 * d + 1) * dm + c:(2 * d + 1) * dm + c + cw] = (lf - hi.astype(F32)).astype(BF16)

    def light_chunks():
        for c in range(0, dm, cw):
            szsg_ref[:, c:c + cw] = proj(_C_Z, c).astype(BF16)
            yield
        for c in range(0, dm, cw):
            v = proj(_C_V, c)
            qkv_ref[:, 3 * dm + c:3 * dm + c + cw] = v.astype(BF16)
            vt = v.T.astype(BF16)
            for j in range(rows // TILE):
                vt_ref[j, c:c + cw, :] = vt[:, j * TILE:(j + 1) * TILE]
            yield
        for c in range(0, dm, cw):
            szsg_ref[:, dm + c:dm + c + cw] = proj(_C_G, c).astype(BF16)
            yield
        for c in range(0, 2 * dm, cw):
            gate_ref[:, c:c + cw] = proj(_C_GATE, c).astype(BF16)
            yield

    light = light_chunks()
    n_conv = SSD_CONV_DIM // cw
    n_light = 5 * dm // cw
    for k in range(n_conv):
        c = k * cw
        old = xe_scr[:, c:c + cw]
        xe = proj(_C_XBC, c, hn_ext)
        xe_scr[:, c:c + cw] = xe
        xhead_ref[:, c:c + cw] = xe[:HALO_ROWS].astype(BF16)
        xmeta_ref[:, c:c + cw] = xe[TILE - N_META:TILE].astype(BF16)
        for _ in range((k + 1) * n_light // n_conv - k * n_light // n_conv):
            next(light)
        act = _conv_silu(old[:rows], old[rows:rows + SUBLANES] * keep_prev, old[rows + SUBLANES:] * keep_next,
                         cw_ref[:, c:c + cw], cb_ref[:, c:c + cw])
        act_ref[:, c:c + cw] = act.astype(BF16)
    gated_chunks()
    dt_ref[...] = jax.nn.softplus(_dot(hn, w_ref[:, _C_DT:_C_END]) + dtb_ref[...])


def _inproj(tables, h, nw, w, lb, dtb, cw, cb):
    t = h.shape[0]
    steps = t // MERGE_TILE

    def tile(i):
        return jnp.minimum(i, steps - 1)

    def row(width):
        return pl.BlockSpec((MERGE_TILE, width), lambda i, *_: (tile(i), 0))

    def head(width):
        return pl.BlockSpec((HALO_ROWS, width), lambda i, *_: (tile(i), 0))

    halo = (SUBLANES, D_MODEL)
    in_specs = [row(D_MODEL),
                pl.BlockSpec(halo, lambda i, prev, nxt, *_: (prev[tile(i)], 0)),
                pl.BlockSpec(halo, lambda i, prev, nxt, *_: (nxt[tile(i)], 0)),
                _resident((1, D_MODEL)), _resident((D_MODEL, _C_END)), _resident((1, 2 * D_MODEL)),
                _resident((1, LANES)), _resident((SUBLANES, SSD_CONV_DIM)), _resident((1, SSD_CONV_DIM))]
    out_specs = [row(2 * D_MODEL),
                 pl.BlockSpec((MERGE_TILE, SSD_CONV_DIM), lambda i, *_: (jnp.maximum(i - 1, 0), 0)),
                 head(SSD_CONV_DIM), head(SSD_CONV_DIM), row(4 * D_MODEL),
                 pl.BlockSpec((MERGE_TILE // TILE, D_MODEL, TILE), lambda i, *_: (tile(i), 0, 0)),
                 row(4 * D_MODEL), row(2 * D_MODEL), row(LANES)]
    out_shape = [jax.ShapeDtypeStruct((t, 2 * D_MODEL), BF16), jax.ShapeDtypeStruct((t, SSD_CONV_DIM), BF16),
                 jax.ShapeDtypeStruct((steps * HALO_ROWS, SSD_CONV_DIM), BF16),
                 jax.ShapeDtypeStruct((steps * HALO_ROWS, SSD_CONV_DIM), BF16),
                 jax.ShapeDtypeStruct((t, 4 * D_MODEL), BF16), jax.ShapeDtypeStruct((t // TILE, D_MODEL, TILE), BF16),
                 jax.ShapeDtypeStruct((t, 4 * D_MODEL), BF16),
                 jax.ShapeDtypeStruct((t, 2 * D_MODEL), BF16), jax.ShapeDtypeStruct((t, LANES), F32)]
    grid_spec = pltpu.PrefetchScalarGridSpec(
        num_scalar_prefetch=len(tables), grid=(steps + 1,), in_specs=in_specs, out_specs=out_specs,
        scratch_shapes=[pltpu.VMEM((MERGE_TILE + 2 * SUBLANES, SSD_CONV_DIM), F32)])
    return pl.pallas_call(
        _inproj_body, grid_spec=grid_spec, out_shape=out_shape,
        compiler_params=_params("arbitrary"), name="inproj",
    )(*tables, h, h, h, nw, w, lb, dtb, cw, cb)


def _conv_silu(x, prev8, next8, cw, cb):
    half = SSD_CONV // 2
    sub = lax.broadcasted_iota(jnp.int32, (SUBLANES, 1), 0)
    acc = cb + cw[half:half + 1, :] * x
    for j in range(SSD_CONV):
        sh = half - j
        if sh == 0:
            continue
        rolled = pltpu.roll(x, sh % x.shape[0], axis=0)
        if sh > 0:
            fix = jnp.where(sub < sh, pltpu.roll(prev8, sh, axis=0), rolled[:SUBLANES])
            rolled = jnp.concatenate([fix, rolled[SUBLANES:]], axis=0)
        else:
            fix = jnp.where(sub >= SUBLANES + sh, pltpu.roll(next8, SUBLANES + sh, axis=0), rolled[-SUBLANES:])
            rolled = jnp.concatenate([rolled[:-SUBLANES], fix], axis=0)
        acc = acc + cw[j:j + 1, :] * rolled
    return jax.nn.silu(acc)


def _mix_tile(xs, bm, cm, dtv, qkv_ref, vt_ref, lf3_ref, arow_ref, e_ref, s_ssd, s_hg, *, bwd):
    hpg = SSD_HEADS // SSD_GROUPS
    gw = hpg * SSD_HEADDIM
    cq = HG_CHUNK
    n_chunks = TILE // cq
    chunk_order = list(reversed(range(n_chunks))) if bwd else list(range(n_chunks))
    g_cols = [slice(g * gw, (g + 1) * gw) for g in range(SSD_GROUPS)]
    g_state = [slice(g * SSD_STATE, (g + 1) * SSD_STATE) for g in range(SSD_GROUPS)]
    h_cols = [slice(h * HG_HEADDIM, (h + 1) * HG_HEADDIM) for h in range(HG_HEADS)]

    r_i = lax.broadcasted_iota(jnp.int32, (TILE, TILE), 0)
    c_i = lax.broadcasted_iota(jnp.int32, (TILE, TILE), 1)
    causal = (r_i <= c_i) if bwd else (r_i >= c_i)
    tri = jnp.where(causal, 1.0, 0.0).astype(BF16)
    causal64 = causal[:cq, :cq]
    tri_chunks = jnp.where(jnp.logical_and(causal, r_i // cq == c_i // cq), 1.0, 0.0).astype(BF16)
    dm = D_MODEL
    k0 = (2 if bwd else 1) * dm
    q_all = qkv_ref[:, :dm]
    k_all = qkv_ref[:, k0:k0 + dm]
    v_all = qkv_ref[:, 3 * dm:]
    end_row = 0 if bwd else TILE - 1
    ref_row = (cq - 1 - cq // 2) if bwd else cq // 2
    end64 = 0 if bwd else cq - 1

    s_old = s_ssd[...]
    s_old_b = s_old.astype(BF16)
    cbs = [_dot_nt(cm[:, g_state[g]], bm[:, g_state[g]]) for g in range(SSD_GROUPS)]
    y_offs = [_dot(cm[:, g_state[g]], s_old_b[:, g_cols[g]]) for g in range(SSD_GROUPS)]
    acs = _dot01_rhs3(tri, dtv * arow_ref[...]) * LOG2E
    bcs_tile = (_dot(tri_chunks, lf3_ref[:, :dm]) + _dot(tri_chunks, lf3_ref[:, dm:])) * LOG2E
    bcs_all = [bcs_tile[ci * cq:(ci + 1) * cq] for ci in range(n_chunks)]
    yield

    total = acs[end_row:end_row + 1, :]
    e_in = jnp.exp2(acs)
    e_end = jnp.exp2(total - acs)
    expand = e_ref[...]
    dt_x = _dot(dtv.astype(BF16), expand)
    w_x = _dot((dtv * e_end).astype(BF16), expand)
    ein_x = _dot(e_in.astype(BF16), expand)
    etot_x = _dot01_lhs3(jnp.broadcast_to(jnp.exp2(total), (SUBLANES, LANES)), expand)[0:1, :]
    acs_t = acs.T
    yield

    ca, cb_ = chunk_order
    hg = {}
    for ci in chunk_order:
        rows = slice(ci * cq, (ci + 1) * cq)
        qc, kc, bcs = q_all[rows], k_all[rows], bcs_all[ci]
        bref = bcs[ref_row:ref_row + 1, :]
        btot = bcs[end64:end64 + 1, :]
        qf, kf = qc.astype(F32), kc.astype(F32)
        qe = (qf * jnp.exp2(bcs - bref)).astype(BF16)
        ke = (kf * jnp.exp2(bref - bcs)).astype(BF16)
        kd = (kf * jnp.exp2(btot - bcs)).astype(BF16)
        qd = (qf * jnp.exp2(bcs)).astype(BF16)
        scores = [_dot_nt(qe[:, c], ke[:, c]) for c in h_cols]
        hg[ci] = (scores, kd, qd, jnp.exp2(btot), v_all[rows])
        yield
    scores_a, kd_a, qd_a, decay_a, v_a = hg[ca]
    scores_b, kd_b, qd_b, decay_b, v_b = hg[cb_]
    cross = [_dot_nt(qd_b[:, c], kd_a[:, c]) for c in h_cols]
    in_order = (lambda a, b: [a, b]) if ca < cb_ else (lambda a, b: [b, a])
    qd_tile = jnp.concatenate(in_order(qd_a, qd_b * decay_a.astype(BF16)), axis=0)
    kd_tile = jnp.concatenate(in_order(kd_a * decay_b.astype(BF16), kd_b), axis=0)
    yield

    xdt = (xs * dt_x).astype(BF16)
    xdtd = (xs * w_x).astype(BF16)
    for g in range(SSD_GROUPS):
        s_ssd[:, g_cols[g]] = (s_old[:, g_cols[g]] * etot_x[:, g_cols[g]]
                               + _dot_tn(bm[:, g_state[g]], xdtd[:, g_cols[g]]))
    yield

    st = s_hg[...]
    st_b = st.astype(BF16)
    inter = [_dot_nt(qd_tile[:, c], st_b[:, c]) for c in h_cols]
    upd = [_dot(vt_ref[0, c, :], kd_tile[:, c]) for c in h_cols]
    s_hg[...] = st * (decay_a * decay_b) + jnp.concatenate(upd, axis=1)
    yield
    o_parts = []
    for h, c in enumerate(h_cols):
        intra_a = _dot(jnp.where(causal64, scores_a[h], 0.0).astype(BF16), v_a[:, c])
        intra_b = (_dot(jnp.where(causal64, scores_b[h], 0.0).astype(BF16), v_b[:, c])
                   + _dot(cross[h].astype(BF16), v_a[:, c]))
        o_parts.append(jnp.concatenate(in_order(intra_a, intra_b), axis=0) + inter[h])
    o = jnp.concatenate(o_parts, axis=1)
    yield

    lane_i = lax.broadcasted_iota(jnp.int32, (TILE, LANES), 1)
    lo_half = lane_i < SSD_HEADDIM
    neg_inf = jnp.float32(-jnp.inf)
    y_parts = []
    for g in range(SSD_GROUPS):
        for pair in range(hpg // 2):
            ms = []
            for hh in range(2):
                lane = (SSD_HEADS if bwd else 0) + g * hpg + pair * 2 + hh
                seg = acs[:, lane:lane + 1] - acs_t[lane:lane + 1, :]
                ms.append((cbs[g] * jnp.exp2(jnp.where(causal, seg, neg_inf))).astype(BF16))
            c0 = g * gw + pair * LANES
            xp = xdt[:, c0:c0 + LANES]
            zero = jnp.zeros_like(xp)
            rhs = jnp.concatenate([jnp.where(lo_half, xp, zero), jnp.where(lo_half, zero, xp)], axis=0)
            y_parts.append(_dot(jnp.concatenate(ms, axis=1), rhs))
    y = jnp.concatenate(y_parts, axis=1) + jnp.concatenate(y_offs, axis=1) * ein_x
    return y, o


def _scan_body(f_tile, f_out, f_head, f_first, b_tile, b_out, b_reset,
               actf_ref, xmeta_ref, xhead_ref, dtf_ref, qkvf_ref, vtf_ref, lff_ref,
               actb_ref, dtb_ref, qkvb_ref, vtb_ref, lfb_ref,
               cw_ref, cb_ref, arow_ref, ef_ref, eb_ref,
               yof_ref, yob_ref, sf_ssd, sf_hg, sb_ssd, sb_hg, meta_act):
    i = pl.program_id(0)
    first = f_first[i] == 1

    @pl.when(first)
    def _():
        sf_ssd[...] = jnp.zeros_like(sf_ssd)
        sf_hg[...] = jnp.zeros_like(sf_hg)
        meta_act[...] = _conv_silu(xmeta_ref[...].astype(F32), jnp.zeros((SUBLANES, SSD_CONV_DIM), F32),
                                   xhead_ref[...].astype(F32)[:SUBLANES], cw_ref[...], cb_ref[...]).astype(BF16)

    @pl.when(b_reset[i] == 1)
    def _():
        sb_ssd[...] = jnp.zeros_like(sb_ssd)
        sb_hg[...] = jnp.zeros_like(sb_hg)

    n_pad = TILE - N_META
    act = jnp.concatenate([actf_ref[:n_pad, :], jnp.where(first, meta_act[...], actf_ref[n_pad:, :])], axis=0)
    row = lax.broadcasted_iota(jnp.int32, (TILE, 1), 0)
    valid = jnp.where(jnp.logical_and(first, row < n_pad), 0.0, 1.0)
    dm, gn = D_MODEL, SSD_GN
    directions = [
        _mix_tile(act[:, :dm].astype(F32), act[:, dm:dm + gn], act[:, dm + gn:], dtf_ref[...] * valid,
                  qkvf_ref, vtf_ref, lff_ref, arow_ref, ef_ref, sf_ssd, sf_hg, bwd=False),
        _mix_tile(actb_ref[:, :dm].astype(F32), actb_ref[:, dm:dm + gn], actb_ref[:, dm + gn:], dtb_ref[...],
                  qkvb_ref, vtb_ref, lfb_ref, arow_ref, eb_ref, sb_ssd, sb_hg, bwd=True),
    ]
    results = [None] * len(directions)

    def advance(d):
        if results[d] is None:
            try:
                next(directions[d])
            except StopIteration as done:
                results[d] = done.value

    for _ in range(SCAN_LEAD):
        advance(0)
    while any(r is None for r in results):
        for d in range(len(directions)):
            advance(d)
    for (y, o), yo_ref in zip(results, (yof_ref, yob_ref)):
        yo_ref[:, :dm] = y.astype(BF16)
        yo_ref[:, dm:] = o.astype(BF16)


def _scan(tables, n_spare, meta_step, act, xmeta, xhead, dt, qkv, vt, lf3, cw, cb, arow, expands):
    n_rows = meta_step * MERGE_TILE

    def fwd(width, col=0):
        return pl.BlockSpec((TILE, width), lambda i, f_tile, *_: (f_tile[i], col))

    def bwd(width, col=0):
        return pl.BlockSpec((TILE, width), lambda i, f_tile, f_out, f_head, f_first, b_tile, *_: (b_tile[i], col))

    vt_blk = (1, D_MODEL, TILE)
    vt_fwd = pl.BlockSpec(vt_blk, lambda i, f_tile, *_: (f_tile[i], 0, 0))
    vt_bwd = pl.BlockSpec(vt_blk, lambda i, f_tile, f_out, f_head, f_first, b_tile, *_: (b_tile[i], 0, 0))

    def const(shape):
        return pl.BlockSpec(shape, lambda i, *_: (0,) * len(shape))

    halo = (HALO_ROWS, SSD_CONV_DIM)
    in_specs = [
        fwd(SSD_CONV_DIM),
        pl.BlockSpec(halo, lambda i, *_: (meta_step, 0)),
        pl.BlockSpec(halo, lambda i, f_tile, f_out, f_head, *_: (f_head[i], 0)),
        fwd(LANES), fwd(4 * D_MODEL), vt_fwd, fwd(2 * D_MODEL, 0),
        bwd(SSD_CONV_DIM), bwd(LANES), bwd(4 * D_MODEL), vt_bwd, bwd(2 * D_MODEL, 1),
        const((SUBLANES, SSD_CONV_DIM)), const((1, SSD_CONV_DIM)), const((1, LANES)),
        const((LANES, D_MODEL)), const((LANES, D_MODEL)),
    ]
    out_specs = [
        pl.BlockSpec((TILE, 2 * D_MODEL), lambda i, f_tile, f_out, *_: (f_out[i], 0)),
        pl.BlockSpec((TILE, 2 * D_MODEL),
                     lambda i, f_tile, f_out, f_head, f_first, b_tile, b_out, *_: (b_out[i], 0)),
    ]
    state = [pltpu.VMEM((SSD_STATE, D_MODEL), F32), pltpu.VMEM((HG_HEADDIM, D_MODEL), F32)]
    grid_spec = pltpu.PrefetchScalarGridSpec(
        num_scalar_prefetch=len(tables), grid=(len(tables[0]),), in_specs=in_specs, out_specs=out_specs,
        scratch_shapes=state + state + [pltpu.VMEM((N_META, SSD_CONV_DIM), BF16)])
    yo = [jax.ShapeDtypeStruct((n_rows + n * TILE, 2 * D_MODEL), BF16) for n in n_spare]
    return pl.pallas_call(
        _scan_body, grid_spec=grid_spec, out_shape=yo,
        compiler_params=_params("arbitrary"), name="scan",
    )(*tables, act, xmeta, xhead, dt, qkv, vt, lf3, act, dt, qkv, vt, lf3, cw, cb, arow, *expands)


def _merge_body(yof_ref, yob_ref, xs_ref, szsg_ref, gate_ref, h_ref, dexp_ref, snorm_ref, hnorm_ref,
                wa_ref, wb_ref, wo_ref, o_ref):
    dm = D_MODEL
    y = yof_ref[:, :dm].astype(F32) + yob_ref[:, :dm].astype(F32) + dexp_ref[...] * xs_ref[...].astype(F32)
    y = y * jax.nn.silu(szsg_ref[:, :dm].astype(F32))
    bra = _rms(y, snorm_ref[...]).astype(BF16)
    o_parts = []
    for h in range(HG_HEADS):
        cols = slice(dm + h * HG_HEADDIM, dm + (h + 1) * HG_HEADDIM)
        oh = yof_ref[:, cols].astype(F32) + yob_ref[:, cols].astype(F32)
        o_parts.append(oh * lax.rsqrt(jnp.mean(oh * oh, axis=-1, keepdims=True) + EPS))
    sg = jax.nn.silu(szsg_ref[:, dm:].astype(F32))
    brb = (jnp.concatenate(o_parts, axis=1) * hnorm_ref[...] * sg).astype(BF16)
    a = _dot(bra, wa_ref[...])
    b = _dot(brb, wb_ref[...])
    merged = (jax.nn.sigmoid(gate_ref[:, :dm].astype(F32)) * a
              + jax.nn.sigmoid(gate_ref[:, dm:].astype(F32)) * b)
    o_ref[...] = h_ref[...] + _dot(merged.astype(BF16), wo_ref[...])


def _merge(rows, yo_f, yo_b, act, szsg, gates, h, dexp, snorm, hnorm, wa, wb, wo):
    def row(width):
        return pl.BlockSpec((ROW_TILE, width), lambda i: (i, 0))

    sq = _resident((D_MODEL, D_MODEL))
    vec = _resident((1, D_MODEL))
    return pl.pallas_call(
        _merge_body,
        grid=(rows // ROW_TILE,),
        in_specs=[row(2 * D_MODEL), row(2 * D_MODEL), row(D_MODEL), row(2 * D_MODEL), row(2 * D_MODEL),
                  row(D_MODEL), vec, vec, vec, sq, sq, sq],
        out_specs=row(D_MODEL),
        out_shape=jax.ShapeDtypeStruct((rows, D_MODEL), F32),
        compiler_params=_params("parallel"),
        name="merge",
    )(yo_f, yo_b, act, szsg, gates, h, dexp, snorm, hnorm, wa, wb, wo)


def _scan_tables(seq_rows, meta_row, n_tiles):
    meta_tile = meta_row // TILE
    f_tile, f_out, f_head, f_first, b_tile, b_reset = [], [], [], [], [], []
    row0 = 0
    for seq, rows in enumerate(seq_rows):
        t0, n = row0 // TILE, rows // TILE
        f_tile += [meta_tile] + [t0 + j for j in range(n)]
        f_out += [n_tiles + seq] + [t0 + j for j in range(n)]
        f_head += [row0 // MERGE_TILE] * (n + 1)
        f_first += [1] + [0] * n
        b_tile += [t0 + j for j in reversed(range(n))]
        b_reset += [1] + [0] * (n - 1)
        row0 += rows
    n_idle = len(f_tile) - len(b_tile)
    b_out = b_tile + [n_tiles + j for j in range(n_idle)]
    b_tile = b_tile + [b_tile[-1]] * n_idle
    b_reset = b_reset + [0] * n_idle
    as_i32 = lambda xs: tuple(np.asarray(x, np.int32) for x in xs)
    return as_i32((f_tile, f_out, f_head, f_first, b_tile, b_out, b_reset)), (len(seq_rows), n_idle)


def _inproj_tables(seq_rows, meta_row, n_steps):
    per_step = MERGE_TILE // SUBLANES
    prev, nxt, keep_prev, keep_next = [], [], [], []
    step = 0
    for rows in seq_rows:
        n = rows // MERGE_TILE
        for j in range(n):
            prev.append((meta_row + TILE) // SUBLANES - 1 if j == 0 else (step + j) * per_step - 1)
            nxt.append((step + j + 1) * per_step if j < n - 1 else 0)
            keep_prev.append(1)
            keep_next.append(1 if j < n - 1 else 0)
        step += n
    pad = n_steps - step
    as_i32 = lambda xs: tuple(np.asarray(x + [0] * pad, np.int32) for x in xs)
    return as_i32((prev, nxt, keep_prev, keep_next))


def kernel(x_prompt, x_sample, meta_tokens, ffn1_norm, ffn1_w_gate_up, ffn1_w_down, mix_norm, w_in, ssd_conv_w,
           ssd_conv_b, ssd_dt_bias, ssd_a_log, ssd_d, ssd_norm, ssd_w_proj, hg_lb_table, hg_norm, hg_w_proj, w_out,
           ffn2_norm, ffn2_w_gate_up, ffn2_w_down, final_norm):
    assert len(ffn1_norm) == 1, "single-layer block"
    groups = (x_prompt, x_sample)
    seq_rows = []
    for x in groups:
        b, s, dm = x.shape
        assert dm == D_MODEL and s % MERGE_TILE == 0 and (b * s) % ROW_TILE == 0
        seq_rows += [s] * b
    n_rows = sum(seq_rows)
    xp, xs = (x.reshape(-1, D_MODEL) for x in groups)
    tail = jnp.pad(meta_tokens.astype(F32), ((TILE - N_META, ROW_TILE - TILE), (0, 0)))
    scan_tables, n_spare = _scan_tables(seq_rows, n_rows, n_rows // TILE)

    row = lambda v: v.reshape(1, -1).astype(F32)

    h1 = _ffn_in(xp, xs, tail, row(ffn1_norm[0]), ffn1_w_gate_up[0].astype(BF16), ffn1_w_down[0].astype(BF16))

    pts = np.cumsum((0,) + IN_WIDTHS)
    w_f = w_in[0]
    dt_w = jnp.pad(w_f[:, pts[2]:pts[3]], ((0, 0), (0, LANES - 2 * SSD_HEADS)))
    w_packed = jnp.concatenate([w_f[:, :pts[2]], w_f[:, pts[3]:], dt_w], axis=1).astype(BF16)
    lb = jnp.cumsum(jax.nn.softmax(hg_lb_table.astype(F32), axis=1), axis=1)[:, 0].reshape(1, 2 * D_MODEL)
    dtb = jnp.pad(ssd_dt_bias[0].astype(F32).reshape(1, -1), ((0, 0), (0, LANES - 2 * SSD_HEADS)))
    cw = jnp.pad(ssd_conv_w[0].astype(F32), ((0, SUBLANES - SSD_CONV), (0, 0)))
    cb = row(ssd_conv_b[0])
    in_tables = _inproj_tables(seq_rows, n_rows, (n_rows + ROW_TILE) // MERGE_TILE)
    szsg, act, xhead, xmeta, qkv, vt, lf3, gates, dt = _inproj(in_tables, h1, row(mix_norm[0]), w_packed, lb, dtb,
                                                               cw, cb)

    arow = jnp.pad(-jnp.exp(ssd_a_log[0].astype(F32)).reshape(1, -1), ((0, 0), (0, LANES - 2 * SSD_HEADS)))
    head_of_col = np.arange(D_MODEL) // SSD_HEADDIM
    expands = [jnp.asarray((np.arange(LANES)[:, None] == head_of_col[None, :] + SSD_HEADS * d), BF16)
               for d in range(2)]
    dexp = jnp.repeat(ssd_d[0].astype(F32), SSD_HEADDIM).reshape(1, -1)

    yo_f, yo_b = _scan(scan_tables, n_spare, n_rows // MERGE_TILE, act, xmeta, xhead, dt, qkv, vt, lf3, cw, cb,
                       arow, expands)

    h2 = _merge(n_rows, yo_f, yo_b, act, szsg, gates, h1, dexp, row(ssd_norm[0]), row(hg_norm[0]),
                ssd_w_proj[0].astype(BF16), hg_w_proj[0].astype(BF16), w_out[0].astype(BF16))

    wgu2, wd2 = ffn2_w_gate_up[0].astype(BF16), ffn2_w_down[0].astype(BF16)
    outs, row0 = [], 0
    for x in groups:
        b, s, dm = x.shape
        y = _ffn_out(h2, row0, b * s, row(ffn2_norm[0]), wgu2, wd2, row(final_norm))
        outs.append(y.reshape(b, s, dm))
        row0 += b * s
    return tuple(outs)
```

```python
import functools

import numpy as np
import jax
import jax.numpy as jnp
from jax import lax
from jax.experimental import pallas as pl
from jax.experimental.pallas import tpu as pltpu

F32 = jnp.float32
BF16 = jnp.bfloat16

D_MODEL = 1024
N_META = 16
EPS = 1e-6
D_FF = 2816
SSD_HEADS = 16
SSD_HEADDIM = 64
SSD_GROUPS = 4
SSD_STATE = 128
SSD_CONV = 5
SSD_GN = SSD_GROUPS * SSD_STATE
SSD_CONV_DIM = D_MODEL + 2 * SSD_GN
HG_HEADS = 8
HG_HEADDIM = 128
HG_CHUNK = 64
IN_WIDTHS = (D_MODEL, SSD_CONV_DIM, 2 * SSD_HEADS, D_MODEL, 2 * D_MODEL, D_MODEL, D_MODEL, 2 * D_MODEL)

TILE = 128
ROW_TILE = 512
INPROJ_TILE = 256
FFN_CHUNKS = 11
PROJ_CHUNK = 256
HALO_ROWS = 16
SUBLANES = 8
LANES = 128
VMEM_LIMIT = 56 * 1024 * 1024
LOG2E = 1.4426950408889634
SCAN_LEAD = 5


def _rms(x, w):
    return x * lax.rsqrt(jnp.mean(x * x, axis=-1, keepdims=True) + EPS) * w


def _dot(a, b):
    return jnp.dot(a, b, preferred_element_type=F32)


def _dot_nt(a, b):
    return lax.dot_general(a, b, (((1,), (1,)), ((), ())), preferred_element_type=F32)


def _dot_tn(a, b):
    return lax.dot_general(a, b, (((0,), (0,)), ((), ())), preferred_element_type=F32)


def _split3(x):
    x1 = x.astype(BF16)
    r1 = x - x1.astype(F32)
    x2 = r1.astype(BF16)
    x3 = (r1 - x2.astype(F32)).astype(BF16)
    return x1, x2, x3


def _dot01_rhs3(m01, x):
    x1, x2, x3 = _split3(x)
    return _dot(m01, x1) + _dot(m01, x2) + _dot(m01, x3)


def _dot01_lhs3(x, m01):
    x1, x2, x3 = _split3(x)
    return _dot(x1, m01) + _dot(x2, m01) + _dot(x3, m01)


def _resident(shape):
    return pl.BlockSpec(shape, lambda i, *_: (0,) * len(shape), pipeline_mode=pl.Buffered(1))


def _params(semantics):
    return pltpu.CompilerParams(dimension_semantics=(semantics,), vmem_limit_bytes=VMEM_LIMIT)


def _ffn_compute(x, nw_ref, wgu_ref, wd_ref, n_chunks):
    hn = _rms(x, nw_ref[...]).astype(BF16)
    tf = D_FF // n_chunks
    acc = jnp.zeros(x.shape, F32)
    for c in range(n_chunks):
        g = _dot(hn, wgu_ref[:, c * tf:(c + 1) * tf])
        u = _dot(hn, wgu_ref[:, D_FF + c * tf:D_FF + (c + 1) * tf])
        a = (jax.nn.silu(g) * u).astype(BF16)
        acc = acc + _dot(a, wd_ref[c * tf:(c + 1) * tf, :])
    return x + 0.5 * acc


def _ffn_in_body(xp_ref, xs_ref, tail_ref, nw_ref, wgu_ref, wd_ref, o_ref, *, steps_p, steps_s):
    i = pl.program_id(0)
    x = jnp.where(i < steps_p, xp_ref[...], jnp.where(i < steps_p + steps_s, xs_ref[...], tail_ref[...]))
    o_ref[...] = _ffn_compute(x, nw_ref, wgu_ref, wd_ref, FFN_CHUNKS)


def _ffn_in(xp, xs, tail, nw, wgu, wd):
    steps_p, steps_s = xp.shape[0] // ROW_TILE, xs.shape[0] // ROW_TILE
    steps = steps_p + steps_s + 1
    blk = (ROW_TILE, D_MODEL)
    return pl.pallas_call(
        functools.partial(_ffn_in_body, steps_p=steps_p, steps_s=steps_s),
        grid=(steps,),
        in_specs=[pl.BlockSpec(blk, lambda i: (jnp.minimum(i, steps_p - 1), 0)),
                  pl.BlockSpec(blk, lambda i: (jnp.clip(i - steps_p, 0, steps_s - 1), 0)),
                  pl.BlockSpec(blk, lambda i: (0, 0)),
                  _resident((1, D_MODEL)), _resident((D_MODEL, 2 * D_FF)), _resident((D_FF, D_MODEL))],
        out_specs=pl.BlockSpec(blk, lambda i: (i, 0)),
        out_shape=jax.ShapeDtypeStruct((steps * ROW_TILE, D_MODEL), F32),
        compiler_params=_params("parallel"),
        name="ffn_in",
    )(xp, xs, tail, nw, wgu, wd)


def _ffn_out_body(x_ref, nw_ref, wgu_ref, wd_ref, fnw_ref, o_ref):
    o_ref[...] = _rms(_ffn_compute(x_ref[...], nw_ref, wgu_ref, wd_ref, FFN_CHUNKS), fnw_ref[...])


def _ffn_out(h, row0, rows, nw, wgu, wd, fnw):
    blk = (ROW_TILE, D_MODEL)
    off = row0 // ROW_TILE
    return pl.pallas_call(
        _ffn_out_body,
        grid=(rows // ROW_TILE,),
        in_specs=[pl.BlockSpec(blk, lambda i: (i + off, 0)),
                  _resident((1, D_MODEL)), _resident((D_MODEL, 2 * D_FF)), _resident((D_FF, D_MODEL)),
                  _resident((1, D_MODEL))],
        out_specs=pl.BlockSpec(blk, lambda i: (i, 0)),
        out_shape=jax.ShapeDtypeStruct((rows, D_MODEL), F32),
        compiler_params=_params("parallel"),
        name="ffn_out",
    )(h, nw, wgu, wd, fnw)


_C_Z, _C_XBC, _C_Q, _C_F, _C_V, _C_G, _C_GATE, _C_DT, _C_END = (
    0, 1024, 3072, 4096, 6144, 7168, 8192, 10240, 10368)


def _inproj_body(prev_tbl, next_tbl, keep_prev_tbl, keep_next_tbl,
                 h_ref, hprev_ref, hnext_ref, nw_ref, w_ref, lb_ref, dtb_ref, cw_ref, cb_ref,
                 szsg_ref, act_ref, xhead_ref, xmeta_ref, qkv_ref, vt_ref, lf3_ref, gate_ref, dt_ref, xe_scr):
    i = pl.program_id(0)
    i_conv = jnp.maximum(i - 1, 0)
    rows = h_ref.shape[0]

    @pl.when(i == 0)
    def _():
        xe_scr[...] = jnp.zeros_like(xe_scr)
    h_ext = jnp.concatenate([h_ref[...], hprev_ref[...], hnext_ref[...]], axis=0)
    hn_ext = _rms(h_ext, nw_ref[...]).astype(BF16)
    hn = hn_ext[:rows]
    dm = D_MODEL

    cw = PROJ_CHUNK
    keep_prev = jnp.where(keep_prev_tbl[i_conv] == 1, 1.0, 0.0)
    keep_next = jnp.where(keep_next_tbl[i_conv] == 1, 1.0, 0.0)

    def proj(w0, c, lhs=hn):
        return _dot(lhs, w_ref[:, w0 + c:w0 + c + cw])

    def gated_chunks():
        for c in range(0, dm, cw):
            qkv_ref[:, c:c + cw] = jax.nn.silu(proj(_C_Q, c)).astype(BF16)
        for d in range(2):
            for c in range(0, dm, cw):
                lb = lb_ref[:, d * dm + c:d * dm + c + cw]
                f = lb + (1.0 - lb) * jax.nn.sigmoid(proj(_C_F + d * dm, c))
                qkv_ref[:, (1 + d) * dm + c:(1 + d) * dm + c + cw] = (1.0 - f).astype(BF16)
                lf = jnp.log(f)
                hi = lf.astype(BF16)
                lf3_ref[:, 2 * d * dm + c:2 * d * dm + c + cw] = hi
                lf3_ref[:, (2 * d + 1) * dm + c:(2 * d + 1) * dm + c + cw] = (lf - hi.astype(F32)).astype(BF16)

    def light_chunks():
        for c in range(0, dm, cw):
            szsg_ref[:, c:c + cw] = proj(_C_Z, c).astype(BF16)
            yield
        for c in range(0, dm, cw):
            v = proj(_C_V, c)
            qkv_ref[:, 3 * dm + c:3 * dm + c + cw] = v.astype(BF16)
            vt = v.T.astype(BF16)
            for j in range(rows // TILE):
                vt_ref[j, c:c + cw, :] = vt[:, j * TILE:(j + 1) * TILE]
            yield
        for c in range(0, dm, cw):
            szsg_ref[:, dm + c:dm + c + cw] = proj(_C_G, c).astype(BF16)
            yield
        for c in range(0, 2 * dm, cw):
            gate_ref[:, c:c + cw] = proj(_C_GATE, c).astype(BF16)
            yield

    light = light_chunks()
    n_conv = SSD_CONV_DIM // cw
    n_light = 5 * dm // cw
    for k in range(n_conv):
        c = k * cw
        old = xe_scr[:, c:c + cw]
        xe = proj(_C_XBC, c, hn_ext)
        xe_scr[:, c:c + cw] = xe
        xhead_ref[:, c:c + cw] = xe[:HALO_ROWS].astype(BF16)
        xmeta_ref[:, c:c + cw] = xe[TILE - N_META:TILE].astype(BF16)
        for _ in range((k + 1) * n_light // n_conv - k * n_light // n_conv):
            next(light)
        act = _conv_silu(old[:rows], old[rows:rows + SUBLANES] * keep_prev, old[rows + SUBLANES:] * keep_next,
                         cw_ref[:, c:c + cw], cb_ref[:, c:c + cw])
        act_ref[:, c:c + cw] = act.astype(BF16)
    gated_chunks()
    dt_ref[...] = jax.nn.softplus(_dot(hn, w_ref[:, _C_DT:_C_END]) + dtb_ref[...])


def _inproj(tables, h, nw, w, lb, dtb, cw, cb):
    t = h.shape[0]
    steps = t // INPROJ_TILE

    def tile(i):
        return jnp.minimum(i, steps - 1)

    def row(width):
        return pl.BlockSpec((INPROJ_TILE, width), lambda i, *_: (tile(i), 0))

    def head(width):
        return pl.BlockSpec((HALO_ROWS, width), lambda i, *_: (tile(i), 0))

    halo = (SUBLANES, D_MODEL)
    in_specs = [row(D_MODEL),
                pl.BlockSpec(halo, lambda i, prev, nxt, *_: (prev[tile(i)], 0)),
                pl.BlockSpec(halo, lambda i, prev, nxt, *_: (nxt[tile(i)], 0)),
                _resident((1, D_MODEL)), _resident((D_MODEL, _C_END)), _resident((1, 2 * D_MODEL)),
                _resident((1, LANES)), _resident((SUBLANES, SSD_CONV_DIM)), _resident((1, SSD_CONV_DIM))]
    out_specs = [row(2 * D_MODEL),
                 pl.BlockSpec((INPROJ_TILE, SSD_CONV_DIM), lambda i, *_: (jnp.maximum(i - 1, 0), 0)),
                 head(SSD_CONV_DIM), head(SSD_CONV_DIM), row(4 * D_MODEL),
                 pl.BlockSpec((INPROJ_TILE // TILE, D_MODEL, TILE), lambda i, *_: (tile(i), 0, 0)),
                 row(4 * D_MODEL), row(2 * D_MODEL), row(LANES)]
    out_shape = [jax.ShapeDtypeStruct((t, 2 * D_MODEL), BF16), jax.ShapeDtypeStruct((t, SSD_CONV_DIM), BF16),
                 jax.ShapeDtypeStruct((steps * HALO_ROWS, SSD_CONV_DIM), BF16),
                 jax.ShapeDtypeStruct((steps * HALO_ROWS, SSD_CONV_DIM), BF16),
                 jax.ShapeDtypeStruct((t, 4 * D_MODEL), BF16), jax.ShapeDtypeStruct((t // TILE, D_MODEL, TILE), BF16),
                 jax.ShapeDtypeStruct((t, 4 * D_MODEL), BF16),
                 jax.ShapeDtypeStruct((t, 2 * D_MODEL), BF16), jax.ShapeDtypeStruct((t, LANES), F32)]
    grid_spec = pltpu.PrefetchScalarGridSpec(
        num_scalar_prefetch=len(tables), grid=(steps + 1,), in_specs=in_specs, out_specs=out_specs,
        scratch_shapes=[pltpu.VMEM((INPROJ_TILE + 2 * SUBLANES, SSD_CONV_DIM), F32)])
    return pl.pallas_call(
        _inproj_body, grid_spec=grid_spec, out_shape=out_shape,
        compiler_params=_params("arbitrary"), name="inproj",
    )(*tables, h, h, h, nw, w, lb, dtb, cw, cb)


def _conv_silu(x, prev8, next8, cw, cb):
    half = SSD_CONV // 2
    sub = lax.broadcasted_iota(jnp.int32, (SUBLANES, 1), 0)
    acc = cb + cw[half:half + 1, :] * x
    for j in range(SSD_CONV):
        sh = half - j
        if sh == 0:
            continue
        rolled = pltpu.roll(x, sh % x.shape[0], axis=0)
        if sh > 0:
            fix = jnp.where(sub < sh, pltpu.roll(prev8, sh, axis=0), rolled[:SUBLANES])
            rolled = jnp.concatenate([fix, rolled[SUBLANES:]], axis=0)
        else:
            fix = jnp.where(sub >= SUBLANES + sh, pltpu.roll(next8, SUBLANES + sh, axis=0), rolled[-SUBLANES:])
            rolled = jnp.concatenate([rolled[:-SUBLANES], fix], axis=0)
        acc = acc + cw[j:j + 1, :] * rolled
    return jax.nn.silu(acc)


def _mix_tile(xs, bm, cm, dtv, qkv_ref, vt_ref, lf3_ref, arow_ref, e_ref, s_ssd, s_hg, *, bwd):
    hpg = SSD_HEADS // SSD_GROUPS
    gw = hpg * SSD_HEADDIM
    cq = HG_CHUNK
    n_chunks = TILE // cq
    chunk_order = list(reversed(range(n_chunks))) if bwd else list(range(n_chunks))
    g_cols = [slice(g * gw, (g + 1) * gw) for g in range(SSD_GROUPS)]
    g_state = [slice(g * SSD_STATE, (g + 1) * SSD_STATE) for g in range(SSD_GROUPS)]
    h_cols = [slice(h * HG_HEADDIM, (h + 1) * HG_HEADDIM) for h in range(HG_HEADS)]

    r_i = lax.broadcasted_iota(jnp.int32, (TILE, TILE), 0)
    c_i = lax.broadcasted_iota(jnp.int32, (TILE, TILE), 1)
    causal = (r_i <= c_i) if bwd else (r_i >= c_i)
    tri = jnp.where(causal, 1.0, 0.0).astype(BF16)
    causal64 = causal[:cq, :cq]
    tri_chunks = jnp.where(jnp.logical_and(causal, r_i // cq == c_i // cq), 1.0, 0.0).astype(BF16)
    dm = D_MODEL
    k0 = (2 if bwd else 1) * dm
    q_all = qkv_ref[:, :dm]
    k_all = qkv_ref[:, k0:k0 + dm]
    v_all = qkv_ref[:, 3 * dm:]
    end_row = 0 if bwd else TILE - 1
    ref_row = (cq - 1 - cq // 2) if bwd else cq // 2
    end64 = 0 if bwd else cq - 1

    s_old = s_ssd[...]
    s_old_b = s_old.astype(BF16)
    cbs = [_dot_nt(cm[:, g_state[g]], bm[:, g_state[g]]) for g in range(SSD_GROUPS)]
    y_offs = [_dot(cm[:, g_state[g]], s_old_b[:, g_cols[g]]) for g in range(SSD_GROUPS)]
    acs = _dot01_rhs3(tri, dtv * arow_ref[...]) * LOG2E
    bcs_tile = (_dot(tri_chunks, lf3_ref[:, :dm]) + _dot(tri_chunks, lf3_ref[:, dm:])) * LOG2E
    bcs_all = [bcs_tile[ci * cq:(ci + 1) * cq] for ci in range(n_chunks)]
    yield

    total = acs[end_row:end_row + 1, :]
    e_in = jnp.exp2(acs)
    e_end = jnp.exp2(total - acs)
    expand = e_ref[...]
    dt_x = _dot(dtv.astype(BF16), expand)
    w_x = _dot((dtv * e_end).astype(BF16), expand)
    ein_x = _dot(e_in.astype(BF16), expand)
    etot_x = _dot01_lhs3(jnp.broadcast_to(jnp.exp2(total), (SUBLANES, LANES)), expand)[0:1, :]
    acs_t = acs.T
    yield

    ca, cb_ = chunk_order
    hg = {}
    for ci in chunk_order:
        rows = slice(ci * cq, (ci + 1) * cq)
        qc, kc, bcs = q_all[rows], k_all[rows], bcs_all[ci]
        bref = bcs[ref_row:ref_row + 1, :]
        btot = bcs[end64:end64 + 1, :]
        qf, kf = qc.astype(F32), kc.astype(F32)
        qe = (qf * jnp.exp2(bcs - bref)).astype(BF16)
        ke = (kf * jnp.exp2(bref - bcs)).astype(BF16)
        kd = (kf * jnp.exp2(btot - bcs)).astype(BF16)
        qd = (qf * jnp.exp2(bcs)).astype(BF16)
        scores = [_dot_nt(qe[:, c], ke[:, c]) for c in h_cols]
        hg[ci] = (scores, kd, qd, jnp.exp2(btot), v_all[rows])
        yield
    scores_a, kd_a, qd_a, decay_a, v_a = hg[ca]
    scores_b, kd_b, qd_b, decay_b, v_b = hg[cb_]
    cross = [_dot_nt(qd_b[:, c], kd_a[:, c]) for c in h_cols]
    in_order = (lambda a, b: [a, b]) if ca < cb_ else (lambda a, b: [b, a])
    qd_tile = jnp.concatenate(in_order(qd_a, qd_b * decay_a.astype(BF16)), axis=0)
    kd_tile = jnp.concatenate(in_order(kd_a * decay_b.astype(BF16), kd_b), axis=0)
    yield

    xdt = (xs * dt_x).astype(BF16)
    xdtd = (xs * w_x).astype(BF16)
    for g in range(SSD_GROUPS):
        s_ssd[:, g_cols[g]] = (s_old[:, g_cols[g]] * etot_x[:, g_cols[g]]
                               + _dot_tn(bm[:, g_state[g]], xdtd[:, g_cols[g]]))
    yield

    st = s_hg[...]
    st_b = st.astype(BF16)
    inter = [_dot_nt(qd_tile[:, c], st_b[:, c]) for c in h_cols]
    upd = [_dot(vt_ref[0, c, :], kd_tile[:, c]) for c in h_cols]
    s_hg[...] = st * (decay_a * decay_b) + jnp.concatenate(upd, axis=1)
    yield
    o_parts = []
    for h, c in enumerate(h_cols):
        intra_a = _dot(jnp.where(causal64, scores_a[h], 0.0).astype(BF16), v_a[:, c])
        intra_b = (_dot(jnp.where(causal64, scores_b[h], 0.0).astype(BF16), v_b[:, c])
                   + _dot(cross[h].astype(BF16), v_a[:, c]))
        o_parts.append(jnp.concatenate(in_order(intra_a, intra_b), axis=0) + inter[h])
    o = jnp.concatenate(o_parts, axis=1)
    yield

    lane_i = lax.broadcasted_iota(jnp.int32, (TILE, LANES), 1)
    lo_half = lane_i < SSD_HEADDIM
    neg_inf = jnp.float32(-jnp.inf)
    y_parts = []
    for g in range(SSD_GROUPS):
        for pair in range(hpg // 2):
            ms = []
            for hh in range(2):
                lane = (SSD_HEADS if bwd else 0) + g * hpg + pair * 2 + hh
                seg = acs[:, lane:lane + 1] - acs_t[lane:lane + 1, :]
                ms.append((cbs[g] * jnp.exp2(jnp.where(causal, seg, neg_inf))).astype(BF16))
            c0 = g * gw + pair * LANES
            xp = xdt[:, c0:c0 + LANES]
            zero = jnp.zeros_like(xp)
            rhs = jnp.concatenate([jnp.where(lo_half, xp, zero), jnp.where(lo_half, zero, xp)], axis=0)
            y_parts.append(_dot(jnp.concatenate(ms, axis=1), rhs))
    y = jnp.concatenate(y_parts, axis=1) + jnp.concatenate(y_offs, axis=1) * ein_x
    return y, o


def _scan_body(f_tile, f_out, f_head, f_first, b_tile, b_out, b_reset,
               actf_ref, xmeta_ref, xhead_ref, dtf_ref, qkvf_ref, vtf_ref, lff_ref,
               actb_ref, dtb_ref, qkvb_ref, vtb_ref, lfb_ref,
               cw_ref, cb_ref, arow_ref, ef_ref, eb_ref,
               yof_ref, yob_ref, sf_ssd, sf_hg, sb_ssd, sb_hg, meta_act):
    i = pl.program_id(0)
    first = f_first[i] == 1

    @pl.when(first)
    def _():
        sf_ssd[...] = jnp.zeros_like(sf_ssd)
        sf_hg[...] = jnp.zeros_like(sf_hg)
        meta_act[...] = _conv_silu(xmeta_ref[...].astype(F32), jnp.zeros((SUBLANES, SSD_CONV_DIM), F32),
                                   xhead_ref[...].astype(F32)[:SUBLANES], cw_ref[...], cb_ref[...]).astype(BF16)

    @pl.when(b_reset[i] == 1)
    def _():
        sb_ssd[...] = jnp.zeros_like(sb_ssd)
        sb_hg[...] = jnp.zeros_like(sb_hg)

    n_pad = TILE - N_META
    act = jnp.concatenate([actf_ref[:n_pad, :], jnp.where(first, meta_act[...], actf_ref[n_pad:, :])], axis=0)
    row = lax.broadcasted_iota(jnp.int32, (TILE, 1), 0)
    valid = jnp.where(jnp.logical_and(first, row < n_pad), 0.0, 1.0)
    dm, gn = D_MODEL, SSD_GN
    directions = [
        _mix_tile(act[:, :dm].astype(F32), act[:, dm:dm + gn], act[:, dm + gn:], dtf_ref[...] * valid,
                  qkvf_ref, vtf_ref, lff_ref, arow_ref, ef_ref, sf_ssd, sf_hg, bwd=False),
        _mix_tile(actb_ref[:, :dm].astype(F32), actb_ref[:, dm:dm + gn], actb_ref[:, dm + gn:], dtb_ref[...],
                  qkvb_ref, vtb_ref, lfb_ref, arow_ref, eb_ref, sb_ssd, sb_hg, bwd=True),
    ]
    results = [None] * len(directions)

    def advance(d):
        if results[d] is None:
            try:
                next(directions[d])
            except StopIteration as done:
                results[d] = done.value

    for _ in range(SCAN_LEAD):
        advance(0)
    while any(r is None for r in results):
        for d in range(len(directions)):
            advance(d)
    for (y, o), yo_ref in zip(results, (yof_ref, yob_ref)):
        yo_ref[:, :dm] = y.astype(BF16)
        yo_ref[:, dm:] = o.astype(BF16)


def _scan(tables, n_spare, meta_step, act, xmeta, xhead, dt, qkv, vt, lf3, cw, cb, arow, expands):
    n_rows = meta_step * INPROJ_TILE

    def fwd(width, col=0):
        return pl.BlockSpec((TILE, width), lambda i, f_tile, *_: (f_tile[i], col))

    def bwd(width, col=0):
        return pl.BlockSpec((TILE, width), lambda i, f_tile, f_out, f_head, f_first, b_tile, *_: (b_tile[i], col))

    vt_blk = (1, D_MODEL, TILE)
    vt_fwd = pl.BlockSpec(vt_blk, lambda i, f_tile, *_: (f_tile[i], 0, 0))
    vt_bwd = pl.BlockSpec(vt_blk, lambda i, f_tile, f_out, f_head, f_first, b_tile, *_: (b_tile[i], 0, 0))

    def const(shape):
        return pl.BlockSpec(shape, lambda i, *_: (0,) * len(shape))

    halo = (HALO_ROWS, SSD_CONV_DIM)
    in_specs = [
        fwd(SSD_CONV_DIM),
        pl.BlockSpec(halo, lambda i, *_: (meta_step, 0)),
        pl.BlockSpec(halo, lambda i, f_tile, f_out, f_head, *_: (f_head[i], 0)),
        fwd(LANES), fwd(4 * D_MODEL), vt_fwd, fwd(2 * D_MODEL, 0),
        bwd(SSD_CONV_DIM), bwd(LANES), bwd(4 * D_MODEL), vt_bwd, bwd(2 * D_MODEL, 1),
        const((SUBLANES, SSD_CONV_DIM)), const((1, SSD_CONV_DIM)), const((1, LANES)),
        const((LANES, D_MODEL)), const((LANES, D_MODEL)),
    ]
    out_specs = [
        pl.BlockSpec((TILE, 2 * D_MODEL), lambda i, f_tile, f_out, *_: (f_out[i], 0)),
        pl.BlockSpec((TILE, 2 * D_MODEL),
                     lambda i, f_tile, f_out, f_head, f_first, b_tile, b_out, *_: (b_out[i], 0)),
    ]
    state = [pltpu.VMEM((SSD_STATE, D_MODEL), F32), pltpu.VMEM((HG_HEADDIM, D_MODEL), F32)]
    grid_spec = pltpu.PrefetchScalarGridSpec(
        num_scalar_prefetch=len(tables), grid=(len(tables[0]),), in_specs=in_specs, out_specs=out_specs,
        scratch_shapes=state + state + [pltpu.VMEM((N_META, SSD_CONV_DIM), BF16)])
    yo = [jax.ShapeDtypeStruct((n_rows + n * TILE, 2 * D_MODEL), BF16) for n in n_spare]
    return pl.pallas_call(
        _scan_body, grid_spec=grid_spec, out_shape=yo,
        compiler_params=_params("arbitrary"), name="scan",
    )(*tables, act, xmeta, xhead, dt, qkv, vt, lf3, act, dt, qkv, vt, lf3, cw, cb, arow, *expands)


def _merge_body(yof_ref, yob_ref, xs_ref, szsg_ref, gate_ref, h_ref, dexp_ref, snorm_ref, hnorm_ref,
                wa_ref, wb_ref, wo_ref, o_ref):
    dm = D_MODEL
    y = yof_ref[:, :dm].astype(F32) + yob_ref[:, :dm].astype(F32) + dexp_ref[...] * xs_ref[...].astype(F32)
    y = y * jax.nn.silu(szsg_ref[:, :dm].astype(F32))
    bra = _rms(y, snorm_ref[...]).astype(BF16)
    o_parts = []
    for h in range(HG_HEADS):
        cols = slice(dm + h * HG_HEADDIM, dm + (h + 1) * HG_HEADDIM)
        oh = yof_ref[:, cols].astype(F32) + yob_ref[:, cols].astype(F32)
        o_parts.append(oh * lax.rsqrt(jnp.mean(oh * oh, axis=-1, keepdims=True) + EPS))
    sg = jax.nn.silu(szsg_ref[:, dm:].astype(F32))
    brb = (jnp.concatenate(o_parts, axis=1) * hnorm_ref[...] * sg).astype(BF16)
    a = _dot(bra, wa_ref[...])
    b = _dot(brb, wb_ref[...])
    merged = (jax.nn.sigmoid(gate_ref[:, :dm].astype(F32)) * a
              + jax.nn.sigmoid(gate_ref[:, dm:].astype(F32)) * b)
    o_ref[...] = h_ref[...] + _dot(merged.astype(BF16), wo_ref[...])


def _merge(rows, yo_f, yo_b, act, szsg, gates, h, dexp, snorm, hnorm, wa, wb, wo):
    def row(width):
        return pl.BlockSpec((ROW_TILE, width), lambda i: (i, 0))

    sq = _resident((D_MODEL, D_MODEL))
    vec = _resident((1, D_MODEL))
    return pl.pallas_call(
        _merge_body,
        grid=(rows // ROW_TILE,),
        in_specs=[row(2 * D_MODEL), row(2 * D_MODEL), row(D_MODEL), row(2 * D_MODEL), row(2 * D_MODEL),
                  row(D_MODEL), vec, vec, vec, sq, sq, sq],
        out_specs=row(D_MODEL),
        out_shape=jax.ShapeDtypeStruct((rows, D_MODEL), F32),
        compiler_params=_params("parallel"),
        name="merge",
    )(yo_f, yo_b, act, szsg, gates, h, dexp, snorm, hnorm, wa, wb, wo)


def _scan_tables(seq_rows, meta_row, n_tiles):
    meta_tile = meta_row // TILE
    f_tile, f_out, f_head, f_first, b_tile, b_reset = [], [], [], [], [], []
    row0 = 0
    for seq, rows in enumerate(seq_rows):
        t0, n = row0 // TILE, rows // TILE
        f_tile += [meta_tile] + [t0 + j for j in range(n)]
        f_out += [n_tiles + seq] + [t0 + j for j in range(n)]
        f_head += [row0 // INPROJ_TILE] * (n + 1)
        f_first += [1] + [0] * n
        b_tile += [t0 + j for j in reversed(range(n))]
        b_reset += [1] + [0] * (n - 1)
        row0 += rows
    n_idle = len(f_tile) - len(b_tile)
    b_out = b_tile + [n_tiles + j for j in range(n_idle)]
    b_tile = b_tile + [b_tile[-1]] * n_idle
    b_reset = b_reset + [0] * n_idle
    as_i32 = lambda xs: tuple(np.asarray(x, np.int32) for x in xs)
    return as_i32((f_tile, f_out, f_head, f_first, b_tile, b_out, b_reset)), (len(seq_rows), n_idle)


def _inproj_tables(seq_rows, meta_row, n_steps):
    per_step = INPROJ_TILE // SUBLANES
    prev, nxt, keep_prev, keep_next = [], [], [], []
    step = 0
    for rows in seq_rows:
        n = rows // INPROJ_TILE
        for j in range(n):
            prev.append((meta_row + TILE) // SUBLANES - 1 if j == 0 else (step + j) * per_step - 1)
            nxt.append((step + j + 1) * per_step if j < n - 1 else 0)
            keep_prev.append(1)
            keep_next.append(1 if j < n - 1 else 0)
        step += n
    pad = n_steps - step
    as_i32 = lambda xs: tuple(np.asarray(x + [0] * pad, np.int32) for x in xs)
    return as_i32((prev, nxt, keep_prev, keep_next))


def kernel(x_prompt, x_sample, meta_tokens, ffn1_norm, ffn1_w_gate_up, ffn1_w_down, mix_norm, w_in, ssd_conv_w,
           ssd_conv_b, ssd_dt_bias, ssd_a_log, ssd_d, ssd_norm, ssd_w_proj, hg_lb_table, hg_norm, hg_w_proj, w_out,
           ffn2_norm, ffn2_w_gate_up, ffn2_w_down, final_norm):
    assert len(ffn1_norm) == 1, "single-layer block"
    groups = (x_prompt, x_sample)
    seq_rows = []
    for x in groups:
        b, s, dm = x.shape
        assert dm == D_MODEL and s % INPROJ_TILE == 0 and (b * s) % ROW_TILE == 0
        seq_rows += [s] * b
    n_rows = sum(seq_rows)
    xp, xs = (x.reshape(-1, D_MODEL) for x in groups)
    tail = jnp.pad(meta_tokens.astype(F32), ((TILE - N_META, ROW_TILE - TILE), (0, 0)))
    scan_tables, n_spare = _scan_tables(seq_rows, n_rows, n_rows // TILE)

    row = lambda v: v.reshape(1, -1).astype(F32)

    h1 = _ffn_in(xp, xs, tail, row(ffn1_norm[0]), ffn1_w_gate_up[0].astype(BF16), ffn1_w_down[0].astype(BF16))

    pts = np.cumsum((0,) + IN_WIDTHS)
    w_f = w_in[0]
    dt_w = jnp.pad(w_f[:, pts[2]:pts[3]], ((0, 0), (0, LANES - 2 * SSD_HEADS)))
    w_packed = jnp.concatenate([w_f[:, :pts[2]], w_f[:, pts[3]:], dt_w], axis=1).astype(BF16)
    lb = jnp.cumsum(jax.nn.softmax(hg_lb_table.astype(F32), axis=1), axis=1)[:, 0].reshape(1, 2 * D_MODEL)
    dtb = jnp.pad(ssd_dt_bias[0].astype(F32).reshape(1, -1), ((0, 0), (0, LANES - 2 * SSD_HEADS)))
    cw = jnp.pad(ssd_conv_w[0].astype(F32), ((0, SUBLANES - SSD_CONV), (0, 0)))
    cb = row(ssd_conv_b[0])
    in_tables = _inproj_tables(seq_rows, n_rows, (n_rows + ROW_TILE) // INPROJ_TILE)
    szsg, act, xhead, xmeta, qkv, vt, lf3, gates, dt = _inproj(in_tables, h1, row(mix_norm[0]), w_packed, lb, dtb,
                                                               cw, cb)

    arow = jnp.pad(-jnp.exp(ssd_a_log[0].astype(F32)).reshape(1, -1), ((0, 0), (0, LANES - 2 * SSD_HEADS)))
    head_of_col = np.arange(D_MODEL) // SSD_HEADDIM
    expands = [jnp.asarray((np.arange(LANES)[:, None] == head_of_col[None, :] + SSD_HEADS * d), BF16)
               for d in range(2)]
    dexp = jnp.repeat(ssd_d[0].astype(F32), SSD_HEADDIM).reshape(1, -1)

    yo_f, yo_b = _scan(scan_tables, n_spare, n_rows // INPROJ_TILE, act, xmeta, xhead, dt, qkv, vt, lf3, cw, cb,
                       arow, expands)

    h2 = _merge(n_rows, yo_f, yo_b, act, szsg, gates, h1, dexp, row(ssd_norm[0]), row(hg_norm[0]),
                ssd_w_proj[0].astype(BF16), hg_w_proj[0].astype(BF16), w_out[0].astype(BF16))

    wgu2, wd2 = ffn2_w_gate_up[0].astype(BF16), ffn2_w_down[0].astype(BF16)
    outs, row0 = [], 0
    for x in groups:
        b, s, dm = x.shape
        y = _ffn_out(h2, row0, b * s, row(ffn2_norm[0]), wgu2, wd2, row(final_norm))
        outs.append(y.reshape(b, s, dm))
        row0 += b * s
    return tuple(outs)
```

```python
import functools

import numpy as np
import jax
import jax.numpy as jnp
from jax import lax
from jax.experimental import pallas as pl
from jax.experimental.pallas import tpu as pltpu

F32 = jnp.float32
BF16 = jnp.bfloat16

D_MODEL = 1024
N_META = 16
EPS = 1e-6
D_FF = 2816
SSD_HEADS = 16
SSD_HEADDIM = 64
SSD_GROUPS = 4
SSD_STATE = 128
SSD_CONV = 5
SSD_GN = SSD_GROUPS * SSD_STATE
SSD_CONV_DIM = D_MODEL + 2 * SSD_GN
HG_HEADS = 8
HG_HEADDIM = 128
HG_CHUNK = 64
IN_WIDTHS = (D_MODEL, SSD_CONV_DIM, 2 * SSD_HEADS, D_MODEL, 2 * D_MODEL, D_MODEL, D_MODEL, 2 * D_MODEL)

TILE = 128
ROW_TILE = 512
INPROJ_TILE = 256
FFN_CHUNKS = 11
PROJ_CHUNK = 256
HALO_ROWS = 16
SUBLANES = 8
LANES = 128
VMEM_LIMIT = 56 * 1024 * 1024
LOG2E = 1.4426950408889634
SCAN_LEAD = 5


def _rms(x, w):
    return x * lax.rsqrt(jnp.mean(x * x, axis=-1, keepdims=True) + EPS) * w


def _dot(a, b):
    return jnp.dot(a, b, preferred_element_type=F32)


def _dot_nt(a, b):
    return lax.dot_general(a, b, (((1,), (1,)), ((), ())), preferred_element_type=F32)


def _dot_tn(a, b):
    return lax.dot_general(a, b, (((0,), (0,)), ((), ())), preferred_element_type=F32)


def _split3(x):
    x1 = x.astype(BF16)
    r1 = x - x1.astype(F32)
    x2 = r1.astype(BF16)
    x3 = (r1 - x2.astype(F32)).astype(BF16)
    return x1, x2, x3


def _dot01_rhs3(m01, x):
    x1, x2, x3 = _split3(x)
    return _dot(m01, x1) + _dot(m01, x2) + _dot(m01, x3)


def _dot01_lhs3(x, m01):
    x1, x2, x3 = _split3(x)
    return _dot(x1, m01) + _dot(x2, m01) + _dot(x3, m01)


def _resident(shape):
    return pl.BlockSpec(shape, lambda i, *_: (0,) * len(shape), pipeline_mode=pl.Buffered(1))


def _params(semantics):
    return pltpu.CompilerParams(dimension_semantics=(semantics,), vmem_limit_bytes=VMEM_LIMIT)


def _ffn_compute(x, nw_ref, wgu_ref, wd_ref, n_chunks):
    hn = _rms(x, nw_ref[...]).astype(BF16)
    tf = D_FF // n_chunks
    acc = jnp.zeros(x.shape, F32)
    for c in range(n_chunks):
        g = _dot(hn, wgu_ref[:, c * tf:(c + 1) * tf])
        u = _dot(hn, wgu_ref[:, D_FF + c * tf:D_FF + (c + 1) * tf])
        a = (jax.nn.silu(g) * u).astype(BF16)
        acc = acc + _dot(a, wd_ref[c * tf:(c + 1) * tf, :])
    return x + 0.5 * acc


def _ffn_in_body(xp_ref, xs_ref, tail_ref, nw_ref, wgu_ref, wd_ref, o_ref, *, steps_p, steps_s):
    i = pl.program_id(0)

    def run(x_ref):
        o_ref[...] = _ffn_compute(x_ref[...], nw_ref, wgu_ref, wd_ref, FFN_CHUNKS)

    pl.when(i < steps_p)(lambda: run(xp_ref))
    pl.when(jnp.logical_and(i >= steps_p, i < steps_p + steps_s))(lambda: run(xs_ref))
    pl.when(i >= steps_p + steps_s)(lambda: run(tail_ref))


def _ffn_in(xp, xs, tail, nw, wgu, wd):
    steps_p, steps_s = xp.shape[0] // ROW_TILE, xs.shape[0] // ROW_TILE
    steps = steps_p + steps_s + 1
    blk = (ROW_TILE, D_MODEL)
    return pl.pallas_call(
        functools.partial(_ffn_in_body, steps_p=steps_p, steps_s=steps_s),
        grid=(steps,),
        in_specs=[pl.BlockSpec(blk, lambda i: (jnp.minimum(i, steps_p - 1), 0)),
                  pl.BlockSpec(blk, lambda i: (jnp.clip(i - steps_p, 0, steps_s - 1), 0)),
                  pl.BlockSpec(blk, lambda i: (0, 0)),
                  _resident((1, D_MODEL)), _resident((D_MODEL, 2 * D_FF)), _resident((D_FF, D_MODEL))],
        out_specs=pl.BlockSpec(blk, lambda i: (i, 0)),
        out_shape=jax.ShapeDtypeStruct((steps * ROW_TILE, D_MODEL), F32),
        compiler_params=_params("parallel"),
        name="ffn_in",
    )(xp, xs, tail, nw, wgu, wd)


def _ffn_out_body(x_ref, nw_ref, wgu_ref, wd_ref, fnw_ref, o_ref):
    o_ref[...] = _rms(_ffn_compute(x_ref[...], nw_ref, wgu_ref, wd_ref, FFN_CHUNKS), fnw_ref[...])


def _ffn_out(h, row0, rows, nw, wgu, wd, fnw):
    blk = (ROW_TILE, D_MODEL)
    off = row0 // ROW_TILE
    return pl.pallas_call(
        _ffn_out_body,
        grid=(rows // ROW_TILE,),
        in_specs=[pl.BlockSpec(blk, lambda i: (i + off, 0)),
                  _resident((1, D_MODEL)), _resident((D_MODEL, 2 * D_FF)), _resident((D_FF, D_MODEL)),
                  _resident((1, D_MODEL))],
        out_specs=pl.BlockSpec(blk, lambda i: (i, 0)),
        out_shape=jax.ShapeDtypeStruct((rows, D_MODEL), F32),
        compiler_params=_params("parallel"),
        name="ffn_out",
    )(h, nw, wgu, wd, fnw)


_C_Z, _C_XBC, _C_Q, _C_F, _C_V, _C_G, _C_GATE, _C_DT, _C_END = (
    0, 1024, 3072, 4096, 6144, 7168, 8192, 10240, 10368)


def _inproj_body(prev_tbl, next_tbl, keep_prev_tbl, keep_next_tbl,
                 h_ref, hprev_ref, hnext_ref, nw_ref, w_ref, lb_ref, dtb_ref, cw_ref, cb_ref,
                 szsg_ref, act_ref, xhead_ref, xmeta_ref, qkv_ref, vt_ref, lf3_ref, gate_ref, dt_ref, xe_scr):
    i = pl.program_id(0)
    i_conv = jnp.maximum(i - 1, 0)
    rows = h_ref.shape[0]

    @pl.when(i == 0)
    def _():
        xe_scr[...] = jnp.zeros_like(xe_scr)
    h_ext = jnp.concatenate([h_ref[...], hprev_ref[...], hnext_ref[...]], axis=0)
    hn_ext = _rms(h_ext, nw_ref[...]).astype(BF16)
    hn = hn_ext[:rows]
    dm = D_MODEL

    cw = PROJ_CHUNK
    keep_prev = jnp.where(keep_prev_tbl[i_conv] == 1, 1.0, 0.0)
    keep_next = jnp.where(keep_next_tbl[i_conv] == 1, 1.0, 0.0)

    def proj(w0, c, lhs=hn):
        return _dot(lhs, w_ref[:, w0 + c:w0 + c + cw])

    def gated_chunks():
        for c in range(0, dm, cw):
            qkv_ref[:, c:c + cw] = jax.nn.silu(proj(_C_Q, c)).astype(BF16)
        for d in range(2):
            for c in range(0, dm, cw):
                lb = lb_ref[:, d * dm + c:d * dm + c + cw]
                f = lb + (1.0 - lb) * jax.nn.sigmoid(proj(_C_F + d * dm, c))
                qkv_ref[:, (1 + d) * dm + c:(1 + d) * dm + c + cw] = (1.0 - f).astype(BF16)
                lf = jnp.log(f)
                hi = lf.astype(BF16)
                lf3_ref[:, 2 * d * dm + c:2 * d * dm + c + cw] = hi
                lf3_ref[:, (2 * d + 1) * dm + c:(2 * d + 1) * dm + c + cw] = (lf - hi.astype(F32)).astype(BF16)

    def light_chunks():
        for c in range(0, dm, cw):
            szsg_ref[:, c:c + cw] = proj(_C_Z, c).astype(BF16)
            yield
        for c in range(0, dm, cw):
            v = proj(_C_V, c)
            qkv_ref[:, 3 * dm + c:3 * dm + c + cw] = v.astype(BF16)
            vt = v.T.astype(BF16)
            for j in range(rows // TILE):
                vt_ref[j, c:c + cw, :] = vt[:, j * TILE:(j + 1) * TILE]
            yield
        for c in range(0, dm, cw):
            szsg_ref[:, dm + c:dm + c + cw] = proj(_C_G, c).astype(BF16)
            yield
        for c in range(0, 2 * dm, cw):
            gate_ref[:, c:c + cw] = proj(_C_GATE, c).astype(BF16)
            yield

    light = light_chunks()
    n_conv = SSD_CONV_DIM // cw
    n_light = 5 * dm // cw
    for k in range(n_conv):
        c = k * cw
        old = xe_scr[:, c:c + cw]
        xe = proj(_C_XBC, c, hn_ext)
        xe_scr[:, c:c + cw] = xe
        xhead_ref[:, c:c + cw] = xe[:HALO_ROWS].astype(BF16)
        xmeta_ref[:, c:c + cw] = xe[TILE - N_META:TILE].astype(BF16)
        for _ in range((k + 1) * n_light // n_conv - k * n_light // n_conv):
            next(light)
        act = _conv_silu(old[:rows], old[rows:rows + SUBLANES] * keep_prev, old[rows + SUBLANES:] * keep_next,
                         cw_ref[:, c:c + cw], cb_ref[:, c:c + cw])
        act_ref[:, c:c + cw] = act.astype(BF16)
    gated_chunks()
    dt_ref[...] = jax.nn.softplus(_dot(hn, w_ref[:, _C_DT:_C_END]) + dtb_ref[...])


def _inproj(tables, h, nw, w, lb, dtb, cw, cb):
    t = h.shape[0]
    steps = t // INPROJ_TILE

    def tile(i):
        return jnp.minimum(i, steps - 1)

    def row(width):
        return pl.BlockSpec((INPROJ_TILE, width), lambda i, *_: (tile(i), 0))

    def head(width):
        return pl.BlockSpec((HALO_ROWS, width), lambda i, *_: (tile(i), 0))

    halo = (SUBLANES, D_MODEL)
    in_specs = [row(D_MODEL),
                pl.BlockSpec(halo, lambda i, prev, nxt, *_: (prev[tile(i)], 0)),
                pl.BlockSpec(halo, lambda i, prev, nxt, *_: (nxt[tile(i)], 0)),
                _resident((1, D_MODEL)), _resident((D_MODEL, _C_END)), _resident((1, 2 * D_MODEL)),
                _resident((1, LANES)), _resident((SUBLANES, SSD_CONV_DIM)), _resident((1, SSD_CONV_DIM))]
    out_specs = [row(2 * D_MODEL),
                 pl.BlockSpec((INPROJ_TILE, SSD_CONV_DIM), lambda i, *_: (jnp.maximum(i - 1, 0), 0)),
                 head(SSD_CONV_DIM), head(SSD_CONV_DIM), row(4 * D_MODEL),
                 pl.BlockSpec((INPROJ_TILE // TILE, D_MODEL, TILE), lambda i, *_: (tile(i), 0, 0)),
                 row(4 * D_MODEL), row(2 * D_MODEL), row(LANES)]
    out_shape = [jax.ShapeDtypeStruct((t, 2 * D_MODEL), BF16), jax.ShapeDtypeStruct((t, SSD_CONV_DIM), BF16),
                 jax.ShapeDtypeStruct((steps * HALO_ROWS, SSD_CONV_DIM), BF16),
                 jax.ShapeDtypeStruct((steps * HALO_ROWS, SSD_CONV_DIM), BF16),
                 jax.ShapeDtypeStruct((t, 4 * D_MODEL), BF16), jax.ShapeDtypeStruct((t // TILE, D_MODEL, TILE), BF16),
                 jax.ShapeDtypeStruct((t, 4 * D_MODEL), BF16),
                 jax.ShapeDtypeStruct((t, 2 * D_MODEL), BF16), jax.ShapeDtypeStruct((t, LANES), F32)]
    grid_spec = pltpu.PrefetchScalarGridSpec(
        num_scalar_prefetch=len(tables), grid=(steps + 1,), in_specs=in_specs, out_specs=out_specs,
        scratch_shapes=[pltpu.VMEM((INPROJ_TILE + 2 * SUBLANES, SSD_CONV_DIM), F32)])
    return pl.pallas_call(
        _inproj_body, grid_spec=grid_spec, out_shape=out_shape,
        compiler_params=_params("arbitrary"), name="inproj",
    )(*tables, h, h, h, nw, w, lb, dtb, cw, cb)


def _conv_silu(x, prev8, next8, cw, cb):
    half = SSD_CONV // 2
    sub = lax.broadcasted_iota(jnp.int32, (SUBLANES, 1), 0)
    acc = cb + cw[half:half + 1, :] * x
    for j in range(SSD_CONV):
        sh = half - j
        if sh == 0:
            continue
        rolled = pltpu.roll(x, sh % x.shape[0], axis=0)
        if sh > 0:
            fix = jnp.where(sub < sh, pltpu.roll(prev8, sh, axis=0), rolled[:SUBLANES])
            rolled = jnp.concatenate([fix, rolled[SUBLANES:]], axis=0)
        else:
            fix = jnp.where(sub >= SUBLANES + sh, pltpu.roll(next8, SUBLANES + sh, axis=0), rolled[-SUBLANES:])
            rolled = jnp.concatenate([rolled[:-SUBLANES], fix], axis=0)
        acc = acc + cw[j:j + 1, :] * rolled
    return jax.nn.silu(acc)


def _mix_tile(xs, bm, cm, dtv, qkv_ref, vt_ref, lf3_ref, arow_ref, e_ref, s_ssd, s_hg, *, bwd):
    hpg = SSD_HEADS // SSD_GROUPS
    gw = hpg * SSD_HEADDIM
    cq = HG_CHUNK
    n_chunks = TILE // cq
    chunk_order = list(reversed(range(n_chunks))) if bwd else list(range(n_chunks))
    g_cols = [slice(g * gw, (g + 1) * gw) for g in range(SSD_GROUPS)]
    g_state = [slice(g * SSD_STATE, (g + 1) * SSD_STATE) for g in range(SSD_GROUPS)]
    h_cols = [slice(h * HG_HEADDIM, (h + 1) * HG_HEADDIM) for h in range(HG_HEADS)]

    r_i = lax.broadcasted_iota(jnp.int32, (TILE, TILE), 0)
    c_i = lax.broadcasted_iota(jnp.int32, (TILE, TILE), 1)
    causal = (r_i <= c_i) if bwd else (r_i >= c_i)
    tri = jnp.where(causal, 1.0, 0.0).astype(BF16)
    causal64 = causal[:cq, :cq]
    tri_chunks = jnp.where(jnp.logical_and(causal, r_i // cq == c_i // cq), 1.0, 0.0).astype(BF16)
    dm = D_MODEL
    k0 = (2 if bwd else 1) * dm
    q_all = qkv_ref[:, :dm]
    k_all = qkv_ref[:, k0:k0 + dm]
    v_all = qkv_ref[:, 3 * dm:]
    end_row = 0 if bwd else TILE - 1
    ref_row = (cq - 1 - cq // 2) if bwd else cq // 2
    end64 = 0 if bwd else cq - 1

    s_old = s_ssd[...]
    s_old_b = s_old.astype(BF16)
    cbs = [_dot_nt(cm[:, g_state[g]], bm[:, g_state[g]]) for g in range(SSD_GROUPS)]
    y_offs = [_dot(cm[:, g_state[g]], s_old_b[:, g_cols[g]]) for g in range(SSD_GROUPS)]
    acs = _dot01_rhs3(tri, dtv * arow_ref[...]) * LOG2E
    bcs_tile = (_dot(tri_chunks, lf3_ref[:, :dm]) + _dot(tri_chunks, lf3_ref[:, dm:])) * LOG2E
    bcs_all = [bcs_tile[ci * cq:(ci + 1) * cq] for ci in range(n_chunks)]
    yield

    total = acs[end_row:end_row + 1, :]
    e_in = jnp.exp2(acs)
    e_end = jnp.exp2(total - acs)
    expand = e_ref[...]
    dt_x = _dot(dtv.astype(BF16), expand)
    w_x = _dot((dtv * e_end).astype(BF16), expand)
    ein_x = _dot(e_in.astype(BF16), expand)
    etot_x = _dot01_lhs3(jnp.broadcast_to(jnp.exp2(total), (SUBLANES, LANES)), expand)[0:1, :]
    acs_t = acs.T
    yield

    ca, cb_ = chunk_order
    hg = {}
    for ci in chunk_order:
        rows = slice(ci * cq, (ci + 1) * cq)
        qc, kc, bcs = q_all[rows], k_all[rows], bcs_all[ci]
        bref = bcs[ref_row:ref_row + 1, :]
        btot = bcs[end64:end64 + 1, :]
        qf, kf = qc.astype(F32), kc.astype(F32)
        qe = (qf * jnp.exp2(bcs - bref)).astype(BF16)
        ke = (kf * jnp.exp2(bref - bcs)).astype(BF16)
        kd = (kf * jnp.exp2(btot - bcs)).astype(BF16)
        qd = (qf * jnp.exp2(bcs)).astype(BF16)
        scores = [_dot_nt(qe[:, c], ke[:, c]) for c in h_cols]
        hg[ci] = (scores, kd, qd, jnp.exp2(btot), v_all[rows])
        yield
    scores_a, kd_a, qd_a, decay_a, v_a = hg[ca]
    scores_b, kd_b, qd_b, decay_b, v_b = hg[cb_]
    cross = [_dot_nt(qd_b[:, c], kd_a[:, c]) for c in h_cols]
    in_order = (lambda a, b: [a, b]) if ca < cb_ else (lambda a, b: [b, a])
    qd_tile = jnp.concatenate(in_order(qd_a, qd_b * decay_a.astype(BF16)), axis=0)
    kd_tile = jnp.concatenate(in_order(kd_a * decay_b.astype(BF16), kd_b), axis=0)
    yield

    xdt = (xs * dt_x).astype(BF16)
    xdtd = (xs * w_x).astype(BF16)
    for g in range(SSD_GROUPS):
        s_ssd[:, g_cols[g]] = (s_old[:, g_cols[g]] * etot_x[:, g_cols[g]]
                               + _dot_tn(bm[:, g_state[g]], xdtd[:, g_cols[g]]))
    yield

    st = s_hg[...]
    st_b = st.astype(BF16)
    inter = [_dot_nt(qd_tile[:, c], st_b[:, c]) for c in h_cols]
    upd = [_dot(vt_ref[0, c, :], kd_tile[:, c]) for c in h_cols]
    s_hg[...] = st * (decay_a * decay_b) + jnp.concatenate(upd, axis=1)
    yield
    o_parts = []
    for h, c in enumerate(h_cols):
        intra_a = _dot(jnp.where(causal64, scores_a[h], 0.0).astype(BF16), v_a[:, c])
        intra_b = (_dot(jnp.where(causal64, scores_b[h], 0.0).astype(BF16), v_b[:, c])
                   + _dot(cross[h].astype(BF16), v_a[:, c]))
        o_parts.append(jnp.concatenate(in_order(intra_a, intra_b), axis=0) + inter[h])
    o = jnp.concatenate(o_parts, axis=1)
    yield

    lane_i = lax.broadcasted_iota(jnp.int32, (TILE, LANES), 1)
    lo_half = lane_i < SSD_HEADDIM
    neg_inf = jnp.float32(-jnp.inf)
    y_parts = []
    for g in range(SSD_GROUPS):
        for pair in range(hpg // 2):
            ms = []
            for hh in range(2):
                lane = (SSD_HEADS if bwd else 0) + g * hpg + pair * 2 + hh
                seg = acs[:, lane:lane + 1] - acs_t[lane:lane + 1, :]
                ms.append((cbs[g] * jnp.exp2(jnp.where(causal, seg, neg_inf))).astype(BF16))
            c0 = g * gw + pair * LANES
            xp = xdt[:, c0:c0 + LANES]
            zero = jnp.zeros_like(xp)
            rhs = jnp.concatenate([jnp.where(lo_half, xp, zero), jnp.where(lo_half, zero, xp)], axis=0)
            y_parts.append(_dot(jnp.concatenate(ms, axis=1), rhs))
    y = jnp.concatenate(y_parts, axis=1) + jnp.concatenate(y_offs, axis=1) * ein_x
    return y, o


def _scan_body(f_tile, f_out, f_head, f_first, b_tile, b_out, b_reset,
               actf_ref, xmeta_ref, xhead_ref, dtf_ref, qkvf_ref, vtf_ref, lff_ref,
               actb_ref, dtb_ref, qkvb_ref, vtb_ref, lfb_ref,
               cw_ref, cb_ref, arow_ref, ef_ref, eb_ref,
               yof_ref, yob_ref, sf_ssd, sf_hg, sb_ssd, sb_hg, meta_act):
    i = pl.program_id(0)
    first = f_first[i] == 1

    @pl.when(first)
    def _():
        sf_ssd[...] = jnp.zeros_like(sf_ssd)
        sf_hg[...] = jnp.zeros_like(sf_hg)
        meta_act[...] = _conv_silu(xmeta_ref[...].astype(F32), jnp.zeros((SUBLANES, SSD_CONV_DIM), F32),
                                   xhead_ref[...].astype(F32)[:SUBLANES], cw_ref[...], cb_ref[...]).astype(BF16)

    @pl.when(b_reset[i] == 1)
    def _():
        sb_ssd[...] = jnp.zeros_like(sb_ssd)
        sb_hg[...] = jnp.zeros_like(sb_hg)

    n_pad = TILE - N_META
    act = jnp.concatenate([actf_ref[:n_pad, :], jnp.where(first, meta_act[...], actf_ref[n_pad:, :])], axis=0)
    row = lax.broadcasted_iota(jnp.int32, (TILE, 1), 0)
    valid = jnp.where(jnp.logical_and(first, row < n_pad), 0.0, 1.0)
    dm, gn = D_MODEL, SSD_GN
    directions = [
        _mix_tile(act[:, :dm].astype(F32), act[:, dm:dm + gn], act[:, dm + gn:], dtf_ref[...] * valid,
                  qkvf_ref, vtf_ref, lff_ref, arow_ref, ef_ref, sf_ssd, sf_hg, bwd=False),
        _mix_tile(actb_ref[:, :dm].astype(F32), actb_ref[:, dm:dm + gn], actb_ref[:, dm + gn:], dtb_ref[...],
                  qkvb_ref, vtb_ref, lfb_ref, arow_ref, eb_ref, sb_ssd, sb_hg, bwd=True),
    ]
    results = [None] * len(directions)

    def advance(d):
        if results[d] is None:
            try:
                next(directions[d])
            except StopIteration as done:
                results[d] = done.value

    for _ in range(SCAN_LEAD):
        advance(0)
    while any(r is None for r in results):
        for d in range(len(directions)):
            advance(d)
    for (y, o), yo_ref in zip(results, (yof_ref, yob_ref)):
        yo_ref[:, :dm] = y.astype(BF16)
        yo_ref[:, dm:] = o.astype(BF16)


def _scan(tables, n_spare, meta_step, act, xmeta, xhead, dt, qkv, vt, lf3, cw, cb, arow, expands):
    n_rows = meta_step * INPROJ_TILE

    def fwd(width, col=0):
        return pl.BlockSpec((TILE, width), lambda i, f_tile, *_: (f_tile[i], col))

    def bwd(width, col=0):
        return pl.BlockSpec((TILE, width), lambda i, f_tile, f_out, f_head, f_first, b_tile, *_: (b_tile[i], col))

    vt_blk = (1, D_MODEL, TILE)
    vt_fwd = pl.BlockSpec(vt_blk, lambda i, f_tile, *_: (f_tile[i], 0, 0))
    vt_bwd = pl.BlockSpec(vt_blk, lambda i, f_tile, f_out, f_head, f_first, b_tile, *_: (b_tile[i], 0, 0))

    def const(shape):
        return pl.BlockSpec(shape, lambda i, *_: (0,) * len(shape))

    halo = (HALO_ROWS, SSD_CONV_DIM)
    in_specs = [
        fwd(SSD_CONV_DIM),
        pl.BlockSpec(halo, lambda i, *_: (meta_step, 0)),
        pl.BlockSpec(halo, lambda i, f_tile, f_out, f_head, *_: (f_head[i], 0)),
        fwd(LANES), fwd(4 * D_MODEL), vt_fwd, fwd(2 * D_MODEL, 0),
        bwd(SSD_CONV_DIM), bwd(LANES), bwd(4 * D_MODEL), vt_bwd, bwd(2 * D_MODEL, 1),
        const((SUBLANES, SSD_CONV_DIM)), const((1, SSD_CONV_DIM)), const((1, LANES)),
        const((LANES, D_MODEL)), const((LANES, D_MODEL)),
    ]
    out_specs = [
        pl.BlockSpec((TILE, 2 * D_MODEL), lambda i, f_tile, f_out, *_: (f_out[i], 0)),
        pl.BlockSpec((TILE, 2 * D_MODEL),
                     lambda i, f_tile, f_out, f_head, f_first, b_tile, b_out, *_: (b_out[i], 0)),
    ]
    state = [pltpu.VMEM((SSD_STATE, D_MODEL), F32), pltpu.VMEM((HG_HEADDIM, D_MODEL), F32)]
    grid_spec = pltpu.PrefetchScalarGridSpec(
        num_scalar_prefetch=len(tables), grid=(len(tables[0]),), in_specs=in_specs, out_specs=out_specs,
        scratch_shapes=state + state + [pltpu.VMEM((N_META, SSD_CONV_DIM), BF16)])
    yo = [jax.ShapeDtypeStruct((n_rows + n * TILE, 2 * D_MODEL), BF16) for n in n_spare]
    return pl.pallas_call(
        _scan_body, grid_spec=grid_spec, out_shape=yo,
        compiler_params=_params("arbitrary"), name="scan",
    )(*tables, act, xmeta, xhead, dt, qkv, vt, lf3, act, dt, qkv, vt, lf3, cw, cb, arow, *expands)


def _merge_body(yof_ref, yob_ref, xs_ref, szsg_ref, gate_ref, h_ref, dexp_ref, snorm_ref, hnorm_ref,
                wa_ref, wb_ref, wo_ref, o_ref):
    dm = D_MODEL
    y = yof_ref[:, :dm].astype(F32) + yob_ref[:, :dm].astype(F32) + dexp_ref[...] * xs_ref[...].astype(F32)
    y = y * jax.nn.silu(szsg_ref[:, :dm].astype(F32))
    bra = _rms(y, snorm_ref[...]).astype(BF16)
    o_parts = []
    for h in range(HG_HEADS):
        cols = slice(dm + h * HG_HEADDIM, dm + (h + 1) * HG_HEADDIM)
        oh = yof_ref[:, cols].astype(F32) + yob_ref[:, cols].astype(F32)
        o_parts.append(oh * lax.rsqrt(jnp.mean(oh * oh, axis=-1, keepdims=True) + EPS))
    sg = jax.nn.silu(szsg_ref[:, dm:].astype(F32))
    brb = (jnp.concatenate(o_parts, axis=1) * hnorm_ref[...] * sg).astype(BF16)
    a = _dot(bra, wa_ref[...])
    b = _dot(brb, wb_ref[...])
    merged = (jax.nn.sigmoid(gate_ref[:, :dm].astype(F32)) * a
              + jax.nn.sigmoid(gate_ref[:, dm:].astype(F32)) * b)
    o_ref[...] = h_ref[...] + _dot(merged.astype(BF16), wo_ref[...])


def _merge(rows, yo_f, yo_b, act, szsg, gates, h, dexp, snorm, hnorm, wa, wb, wo):
    def row(width):
        return pl.BlockSpec((ROW_TILE, width), lambda i: (i, 0))

    sq = _resident((D_MODEL, D_MODEL))
    vec = _resident((1, D_MODEL))
    return pl.pallas_call(
        _merge_body,
        grid=(rows // ROW_TILE,),
        in_specs=[row(2 * D_MODEL), row(2 * D_MODEL), row(D_MODEL), row(2 * D_MODEL), row(2 * D_MODEL),
                  row(D_MODEL), vec, vec, vec, sq, sq, sq],
        out_specs=row(D_MODEL),
        out_shape=jax.ShapeDtypeStruct((rows, D_MODEL), F32),
        compiler_params=_params("parallel"),
        name="merge",
    )(yo_f, yo_b, act, szsg, gates, h, dexp, snorm, hnorm, wa, wb, wo)


def _scan_tables(seq_rows, meta_row, n_tiles):
    meta_tile = meta_row // TILE
    f_tile, f_out, f_head, f_first, b_tile, b_reset = [], [], [], [], [], []
    row0 = 0
    for seq, rows in enumerate(seq_rows):
        t0, n = row0 // TILE, rows // TILE
        f_tile += [meta_tile] + [t0 + j for j in range(n)]
        f_out += [n_tiles + seq] + [t0 + j for j in range(n)]
        f_head += [row0 // INPROJ_TILE] * (n + 1)
        f_first += [1] + [0] * n
        b_tile += [t0 + j for j in reversed(range(n))]
        b_reset += [1] + [0] * (n - 1)
        row0 += rows
    n_idle = len(f_tile) - len(b_tile)
    b_out = b_tile + [n_tiles + j for j in range(n_idle)]
    b_tile = b_tile + [b_tile[-1]] * n_idle
    b_reset = b_reset + [0] * n_idle
    as_i32 = lambda xs: tuple(np.asarray(x, np.int32) for x in xs)
    return as_i32((f_tile, f_out, f_head, f_first, b_tile, b_out, b_reset)), (len(seq_rows), n_idle)


def _inproj_tables(seq_rows, meta_row, n_steps):
    per_step = INPROJ_TILE // SUBLANES
    prev, nxt, keep_prev, keep_next = [], [], [], []
    step = 0
    for rows in seq_rows:
        n = rows // INPROJ_TILE
        for j in range(n):
            prev.append((meta_row + TILE) // SUBLANES - 1 if j == 0 else (step + j) * per_step - 1)
            nxt.append((step + j + 1) * per_step if j < n - 1 else 0)
            keep_prev.append(1)
            keep_next.append(1 if j < n - 1 else 0)
        step += n
    pad = n_steps - step
    as_i32 = lambda xs: tuple(np.asarray(x + [0] * pad, np.int32) for x in xs)
    return as_i32((prev, nxt, keep_prev, keep_next))


def kernel(x_prompt, x_sample, meta_tokens, ffn1_norm, ffn1_w_gate_up, ffn1_w_down, mix_norm, w_in, ssd_conv_w,
           ssd_conv_b, ssd_dt_bias, ssd_a_log, ssd_d, ssd_norm, ssd_w_proj, hg_lb_table, hg_norm, hg_w_proj, w_out,
           ffn2_norm, ffn2_w_gate_up, ffn2_w_down, final_norm):
    assert len(ffn1_norm) == 1, "single-layer block"
    groups = (x_prompt, x_sample)
    seq_rows = []
    for x in groups:
        b, s, dm = x.shape
        assert dm == D_MODEL and s % INPROJ_TILE == 0 and (b * s) % ROW_TILE == 0
        seq_rows += [s] * b
    n_rows = sum(seq_rows)
    xp, xs = (x.reshape(-1, D_MODEL) for x in groups)
    tail = jnp.pad(meta_tokens.astype(F32), ((TILE - N_META, ROW_TILE - TILE), (0, 0)))
    scan_tables, n_spare = _scan_tables(seq_rows, n_rows, n_rows // TILE)

    row = lambda v: v.reshape(1, -1).astype(F32)

    h1 = _ffn_in(xp, xs, tail, row(ffn1_norm[0]), ffn1_w_gate_up[0].astype(BF16), ffn1_w_down[0].astype(BF16))

    pts = np.cumsum((0,) + IN_WIDTHS)
    w_f = w_in[0]
    dt_w = jnp.pad(w_f[:, pts[2]:pts[3]], ((0, 0), (0, LANES - 2 * SSD_HEADS)))
    w_packed = jnp.concatenate([w_f[:, :pts[2]], w_f[:, pts[3]:], dt_w], axis=1).astype(BF16)
    lb = jnp.cumsum(jax.nn.softmax(hg_lb_table.astype(F32), axis=1), axis=1)[:, 0].reshape(1, 2 * D_MODEL)
    dtb = jnp.pad(ssd_dt_bias[0].astype(F32).reshape(1, -1), ((0, 0), (0, LANES - 2 * SSD_HEADS)))
    cw = jnp.pad(ssd_conv_w[0].astype(F32), ((0, SUBLANES - SSD_CONV), (0, 0)))
    cb = row(ssd_conv_b[0])
    in_tables = _inproj_tables(seq_rows, n_rows, (n_rows + ROW_TILE) // INPROJ_TILE)
    szsg, act, xhead, xmeta, qkv, vt, lf3, gates, dt = _inproj(in_tables, h1, row(mix_norm[0]), w_packed, lb, dtb,
                                                               cw, cb)

    arow = jnp.pad(-jnp.exp(ssd_a_log[0].astype(F32)).reshape(1, -1), ((0, 0), (0, LANES - 2 * SSD_HEADS)))
    head_of_col = np.arange(D_MODEL) // SSD_HEADDIM
    expands = [jnp.asarray((np.arange(LANES)[:, None] == head_of_col[None, :] + SSD_HEADS * d), BF16)
               for d in range(2)]
    dexp = jnp.repeat(ssd_d[0].astype(F32), SSD_HEADDIM).reshape(1, -1)

    yo_f, yo_b = _scan(scan_tables, n_spare, n_rows // INPROJ_TILE, act, xmeta, xhead, dt, qkv, vt, lf3, cw, cb,
                       arow, expands)

    h2 = _merge(n_rows, yo_f, yo_b, act, szsg, gates, h1, dexp, row(ssd_norm[0]), row(hg_norm[0]),
                ssd_w_proj[0].astype(BF16), hg_w_proj[0].astype(BF16), w_out[0].astype(BF16))

    wgu2, wd2 = ffn2_w_gate_up[0].astype(BF16), ffn2_w_down[0].astype(BF16)
    outs, row0 = [], 0
    for x in groups:
        b, s, dm = x.shape
        y = _ffn_out(h2, row0, b * s, row(ffn2_norm[0]), wgu2, wd2, row(final_norm))
        outs.append(y.reshape(b, s, dm))
        row0 += b * s
    return tuple(outs)
```
